```python
import jax, jax.numpy as jnp
from jax import lax
import numpy as np

D_MODEL = 1024
BATCH = 16
SEQ = 2048
DEPTH = 1

CHUNK = 64
PLE_DIM = 256
CONV_WIDTH = D_MODEL // 2
CONV_GROUPS = 8
CONV_K = 3
RWKV_WIDTH = D_MODEL - CONV_WIDTH
RWKV_HEAD = 64
RWKV_HEADS = RWKV_WIDTH // RWKV_HEAD
DECAY_LORA = 64
AICL_LORA = 64
GATE_LORA = 128
D_FF = 2816
FFN_CONV_K = 3
NORM_EPS = 1e-6
GN_EPS = 64e-5
RWKV_COLS = 3 * RWKV_WIDTH + DECAY_LORA + AICL_LORA + GATE_LORA
IN_COLS = 3 * CONV_WIDTH + RWKV_COLS

kernel_name = "hymba_shortconv_rwkv7_convffn_ple"


def rms_norm(x, g):
    xf = x.astype(jnp.float32)
    y = xf * lax.rsqrt(jnp.mean(xf * xf, axis=-1, keepdims=True) + NORM_EPS)
    return (y * g.astype(jnp.float32)).astype(x.dtype)


def causal_dwconv(x, w):
    k_width = w.shape[0]
    s = x.shape[1]
    xp = jnp.pad(x, ((0, 0), (k_width - 1, 0), (0, 0)))
    y = xp[:, k_width - 1:k_width - 1 + s] * w[k_width - 1]
    for j in range(k_width - 1):
        y = y + xp[:, j:j + s] * w[j]
    return y


def token_shift(z):
    return jnp.pad(z, ((0, 0), (1, 0), (0, 0)))[:, :-1]


def short_conv_mixer(z, conv_w):
    x_in, b_gate, c_gate = jnp.split(z, 3, axis=-1)
    return b_gate * causal_dwconv(c_gate * x_in, conv_w)


def wkv7_scan(r, w, k, v, kk, a):
    b, s, h, n = r.shape
    n_chunks = s // CHUNK

    def to_chunks(t):
        return t.reshape(b, n_chunks, CHUNK, h, n).transpose(1, 2, 0, 3, 4)

    xs = (to_chunks(r), to_chunks(w), to_chunks(k), to_chunks(v), to_chunks(kk), to_chunks(a))

    def step(state, inp):
        r_t, w_t, k_t, v_t, kk_t, a_t = inp
        sa = jnp.einsum('bhvk,bhk->bhv', state, -kk_t)
        state = (state * w_t[:, :, None, :]
                 + sa[..., None] * (kk_t * a_t)[:, :, None, :]
                 + v_t[..., None] * k_t[:, :, None, :])
        out = jnp.einsum('bhvk,bhk->bhv', state, r_t)
        return state, out

    def chunk_step(state, chunk_inp):
        return lax.scan(step, state, chunk_inp)

    state0 = jnp.zeros((b, h, n, n), jnp.float32)
    _, out = lax.scan(chunk_step, state0, xs)
    return out.transpose(2, 0, 1, 3, 4).reshape(b, s, h, n)


def rwkv7_mixer(z, mu, w0, w_up, a0, a_up, g_up, k_k, k_a, r_k, gn_w, gn_b):
    dtype = z.dtype
    bsz, s, _ = z.shape
    z = z.astype(jnp.float32)
    z = z + (token_shift(z) - z) * mu.astype(jnp.float32)
    c1 = RWKV_WIDTH
    r, k, v, wd, ad, gd = jnp.split(
        z, [c1, 2 * c1, 3 * c1, 3 * c1 + DECAY_LORA, 3 * c1 + DECAY_LORA + AICL_LORA], axis=-1)
    f32 = lambda t: t.astype(jnp.float32)
    w_log = -jax.nn.softplus(-(f32(w0) + jnp.tanh(wd) @ f32(w_up))) - 0.5
    decay = jnp.exp(-jnp.exp(w_log))
    a = jax.nn.sigmoid(f32(a0) + ad @ f32(a_up))
    g = jax.nn.sigmoid(gd) @ f32(g_up)
    heads = lambda t: t.reshape(bsz, s, RWKV_HEADS, RWKV_HEAD)
    kk = heads(k * f32(k_k))
    kk = kk * lax.rsqrt(jnp.maximum(jnp.sum(kk * kk, axis=-1, keepdims=True), 1e-24))
    k = k * (1.0 + (a - 1.0) * f32(k_a))
    rh, kh, vh, ah, wh = heads(r), heads(k), heads(v), heads(a), heads(decay)
    o = wkv7_scan(rh, wh, kh, vh, kk, ah)
    mean = jnp.mean(o, axis=-1, keepdims=True)
    var = jnp.mean(jnp.square(o - mean), axis=-1, keepdims=True)
    o = (o - mean) * lax.rsqrt(var + GN_EPS)
    o = o * heads(jnp.broadcast_to(f32(gn_w), (bsz, s, RWKV_WIDTH))) + heads(
        jnp.broadcast_to(f32(gn_b), (bsz, s, RWKV_WIDTH)))
    bonus = jnp.sum(rh * kh * f32(r_k), axis=-1, keepdims=True) * vh
    o = (o + bonus).reshape(bsz, s, RWKV_WIDTH) * g
    return o.astype(dtype)


def conv_glu_ffn(h, w_up, conv_w, conv_b, w_down):
    u = causal_dwconv(h @ w_up, conv_w) + conv_b
    gate, val = jnp.split(u, 2, axis=-1)
    return (jax.nn.silu(gate) * val) @ w_down


def setup_inputs(seed: int = 0) -> dict:
    key = jax.random.key(seed)
    ks = jax.random.split(key, 32)
    n = lambda k, shape, scale: jax.random.normal(k, shape, jnp.float32) * scale
    gain = lambda k, shape: 1.0 + 0.02 * jax.random.normal(k, shape, jnp.float32)
    L = DEPTH
    return {
        "x": n(ks[0], (BATCH, SEQ, D_MODEL), 1.0),
        "p": n(ks[1], (DEPTH, BATCH, SEQ, PLE_DIM), 1.0),
        "mix_norm_g": gain(ks[2], (L, D_MODEL)),
        "w_in": n(ks[3], (L, D_MODEL, IN_COLS), D_MODEL ** -0.5),
        "conv_mix_w": n(ks[4], (L, CONV_K, CONV_WIDTH), CONV_K ** -0.5),
        "rwkv_mu": jax.random.uniform(ks[5], (L, RWKV_COLS), jnp.float32),
        "rwkv_w0": jax.random.uniform(ks[6], (L, RWKV_WIDTH), jnp.float32, -4.0, 1.0),
        "rwkv_w_up": n(ks[7], (L, DECAY_LORA, RWKV_WIDTH), 0.1 * DECAY_LORA ** -0.5),
        "rwkv_a0": n(ks[8], (L, RWKV_WIDTH), 0.5),
        "rwkv_a_up": n(ks[9], (L, AICL_LORA, RWKV_WIDTH), 0.5 * AICL_LORA ** -0.5),
        "rwkv_g_up": n(ks[10], (L, GATE_LORA, RWKV_WIDTH), GATE_LORA ** -0.5),
        "rwkv_k_k": 0.85 + n(ks[11], (L, RWKV_WIDTH), 0.05),
        "rwkv_k_a": 1.0 + n(ks[12], (L, RWKV_WIDTH), 0.05),
        "rwkv_r_k": n(ks[13], (L, RWKV_HEADS, RWKV_HEAD), 0.1),
        "rwkv_gn_w": gain(ks[14], (L, RWKV_WIDTH)),
        "rwkv_gn_b": n(ks[15], (L, RWKV_WIDTH), 0.02),
        "w_out": n(ks[16], (L, D_MODEL, D_MODEL), D_MODEL ** -0.5),
        "ffn_norm_g": gain(ks[17], (L, D_MODEL)),
        "ffn_w_up": n(ks[18], (L, D_MODEL, 2 * D_FF), D_MODEL ** -0.5),
        "ffn_conv_w": n(ks[19], (L, FFN_CONV_K, 2 * D_FF), FFN_CONV_K ** -0.5),
        "ffn_conv_b": n(ks[20], (L, 2 * D_FF), 0.02),
        "ffn_w_down": n(ks[21], (L, D_FF, D_MODEL), D_FF ** -0.5),
        "ple_w_proj": n(ks[22], (L, PLE_DIM, D_MODEL), PLE_DIM ** -0.5),
        "ple_norm_g": gain(ks[23], (L, D_MODEL)),
        "ple_gate_norm_g": gain(ks[24], (L, D_MODEL)),
        "ple_w_gate": n(ks[25], (L, D_MODEL, D_MODEL), D_MODEL ** -0.5),
        "final_norm_g": gain(ks[26], (D_MODEL,)),
    }


def reference(x, p, mix_norm_g, w_in, conv_mix_w, rwkv_mu, rwkv_w0, rwkv_w_up, rwkv_a0,
              rwkv_a_up, rwkv_g_up, rwkv_k_k, rwkv_k_a, rwkv_r_k, rwkv_gn_w, rwkv_gn_b,
              w_out, ffn_norm_g, ffn_w_up, ffn_conv_w, ffn_conv_b, ffn_w_down,
              ple_w_proj, ple_norm_g, ple_gate_norm_g, ple_w_gate, final_norm_g):
    for i in range(DEPTH):
        h = rms_norm(x, mix_norm_g[i])
        z = h @ w_in[i]
        z_conv, z_rwkv = z[..., :3 * CONV_WIDTH], z[..., 3 * CONV_WIDTH:]
        y_conv = short_conv_mixer(z_conv, conv_mix_w[i])
        y_rwkv = rwkv7_mixer(z_rwkv, rwkv_mu[i], rwkv_w0[i], rwkv_w_up[i], rwkv_a0[i],
                             rwkv_a_up[i], rwkv_g_up[i], rwkv_k_k[i], rwkv_k_a[i],
                             rwkv_r_k[i], rwkv_gn_w[i], rwkv_gn_b[i])
        x = x + jnp.concatenate([y_conv, y_rwkv], axis=-1) @ w_out[i]
        x = x + conv_glu_ffn(rms_norm(x, ffn_norm_g[i]), ffn_w_up[i], ffn_conv_w[i],
                             ffn_conv_b[i], ffn_w_down[i])
        e = rms_norm(p[i] @ ple_w_proj[i], ple_norm_g[i])
        gate = jax.nn.sigmoid(rms_norm(x, ple_gate_norm_g[i]) @ ple_w_gate[i])
        x = x + gate * e
    return rms_norm(x, final_norm_g)
```

```python
import functools
import math

import jax
import jax.numpy as jnp
from jax import lax
from jax.experimental import pallas as pl
from jax.experimental.pallas import tpu as pltpu

NORM_EPS = 1e-6
GN_EPS = 64e-5
HEAD = 64
WKV_CHUNK = 64
GROUP = 256
HEADS_PER_GROUP = GROUP // HEAD
HALO = 8
SEQ_TILE = 256
FF_CHUNK = 256
VMEM_LIMIT_BYTES = 56 * 1024 * 1024

F32 = jnp.float32
BF16 = jnp.bfloat16
HIGHEST = lax.Precision.HIGHEST
WKV_PREC = lax.Precision.HIGHEST


def _rms(x, g):
    ms = jnp.mean(x * x, axis=-1, keepdims=True)
    return x * lax.rsqrt(ms + NORM_EPS) * g


def _sigmoid(x):
    return 1.0 / (1.0 + jnp.exp(-x))


def _mm(a, b):
    return jnp.dot(a.astype(BF16), b.astype(BF16), preferred_element_type=F32)


def _mm_hilo(a, b_bf16):
    hi = a.astype(BF16)
    lo = (a - hi.astype(F32)).astype(BF16)
    return (jnp.dot(hi, b_bf16, preferred_element_type=F32)
            + jnp.dot(lo, b_bf16, preferred_element_type=F32))


def _wdot(a, b):
    return jnp.dot(a, b, preferred_element_type=F32, precision=WKV_PREC)


def _wdot_nt(a, b):
    return lax.dot_general(a, b, (((1,), (1,)), ((), ())),
                           preferred_element_type=F32, precision=WKV_PREC)


def _wdot_tn(a, b):
    return lax.dot_general(a, b, (((0,), (0,)), ((), ())),
                           preferred_element_type=F32, precision=WKV_PREC)


def _iota2(shape, dim):
    return lax.broadcasted_iota(jnp.int32, shape, dim)


def _mixer_kernel(x_ref, g_ref, win_ref, cw_ref, mu_ref, w0_ref, wup_ref, a0_ref, aup_ref,
                  gup_ref, kk_ref, ka_ref, rk_ref, gnw_ref, gnb_ref, wout_ref,
                  o_ref, zbuf, cxbuf, s_ref, *, ts, cw, rw):
    n_chunks = ts // WKV_CHUNK
    n_groups = rw // GROUP
    C = WKV_CHUNK

    @pl.when(pl.program_id(1) == 0)
    def _():
        zbuf[0:HALO, :] = jnp.zeros((HALO, zbuf.shape[1]), F32)
        cxbuf[0:HALO, :] = jnp.zeros((HALO, cxbuf.shape[1]), F32)
        s_ref[...] = jnp.zeros(s_ref.shape, F32)

    x = x_ref[0]
    h = _rms(x, g_ref[...])
    z = _mm(h, win_ref[...])

    x_in = z[:, 0:cw]
    b_gate = z[:, cw:2 * cw]
    c_gate = z[:, 2 * cw:3 * cw]
    cx = c_gate * x_in
    cxbuf[HALO:HALO + ts, :] = cx
    conv = (cx * cw_ref[2:3, :]
            + cxbuf[HALO - 1:HALO - 1 + ts, :] * cw_ref[1:2, :]
            + cxbuf[HALO - 2:HALO - 2 + ts, :] * cw_ref[0:1, :])
    y_conv = b_gate * conv
    cxbuf[0:HALO, :] = cxbuf[ts:ts + HALO, :]

    zr = z[:, 3 * cw:]
    zbuf[HALO:HALO + ts, :] = zr
    zs = zbuf[HALO - 1:HALO - 1 + ts, :]
    zbuf[0:HALO, :] = zbuf[ts:ts + HALO, :]
    zz = zr + (zs - zr) * mu_ref[...]
    r = zz[:, 0:rw]
    k = zz[:, rw:2 * rw]
    v = zz[:, 2 * rw:3 * rw]
    lora_in = zz[:, 3 * rw:3 * rw + 128]
    gd = zz[:, 3 * rw + 128:3 * rw + 256]

    u_dec = w0_ref[...] + _mm(jnp.tanh(lora_in), wup_ref[...])
    logw = (-math.exp(-0.5)) * _sigmoid(u_dec)
    a = _sigmoid(a0_ref[...] + _mm(lora_in, aup_ref[...]))
    g = _mm(_sigmoid(gd), gup_ref[...])

    seg = (_iota2((rw, rw), 0) // HEAD == _iota2((rw, rw), 1) // HEAD).astype(BF16)

    kk = k * kk_ref[...]
    kk = kk * lax.rsqrt(jnp.maximum(_mm_hilo(kk * kk, seg), 1e-24))
    k2 = k * (1.0 + (a - 1.0) * ka_ref[...])
    aa = -kk
    bb = kk * a

    ri = _iota2((ts, ts), 0)
    ci = _iota2((ts, ts), 1)
    same_chunk = (ri // C) == (ci // C)
    tri = jnp.where(same_chunk & (ci <= ri), 1.0, 0.0).astype(F32)
    ones_blk = jnp.where(same_chunk, 1.0, 0.0).astype(F32)
    cl = jnp.dot(tri, logw, preferred_element_type=F32, precision=HIGHEST)
    ctot = jnp.dot(ones_blk, logw, preferred_element_type=F32, precision=HIGHEST)
    ec = jnp.exp(cl)
    eci = jnp.exp(-cl)
    ecm = jnp.exp(cl - logw)
    ecl = jnp.exp(ctot - cl)
    pc = jnp.exp(ctot)
    a_t = aa * ecm
    r_t = r * ec
    b_t = bb * eci
    k_t = k2 * eci
    b_h = bb * ecl
    k_h = k2 * ecl

    rs = _iota2((C, GROUP), 0)
    cs = _iota2((C, GROUP), 1) % HEAD
    lower_strict = cs < rs
    lower_incl = cs <= rs
    eye_sbs = jnp.where(cs == rs, 1.0, 0.0).astype(F32)
    rb = _iota2((GROUP, GROUP), 0)
    cb = _iota2((GROUP, GROUP), 1)
    bd_mask = (rb // HEAD) == (cb // HEAD)
    eye_bd = rb == cb

    def bd(y):
        return jnp.where(bd_mask, jnp.concatenate([y] * HEADS_PER_GROUP, axis=0), 0.0)

    o_rows = []
    for c in range(n_chunks):
        o_cols = []
        for gi in range(n_groups):
            sl = (slice(c * C, (c + 1) * C), slice(gi * GROUP, (gi + 1) * GROUP))
            at, rt, bt, kt, bh, kh, vv = (a_t[sl], r_t[sl], b_t[sl], k_t[sl], b_h[sl],
                                          k_h[sl], v[sl])
            ar = jnp.concatenate([at, rt], axis=0)
            a1 = _wdot_nt(ar, bd(bt))
            a2 = _wdot_nt(ar, bd(kt))
            a_ab = jnp.where(lower_strict, a1[:C], 0.0)
            a_rb = jnp.where(lower_incl, a1[C:], 0.0)
            a_ak = jnp.where(lower_strict, a2[:C], 0.0)
            a_rk = jnp.where(lower_incl, a2[C:], 0.0)

            lp = a_ab
            tinv = eye_sbs + a_ab
            n_sq = int(math.log2(C))
            for i in range(n_sq):
                m = bd(lp)
                if i == 0:
                    lp = _wdot(lp, m)
                elif i < n_sq - 1:
                    st = _wdot(jnp.concatenate([lp, tinv], axis=0), m)
                    lp = st[:C]
                    tinv = tinv + st[C:]
                else:
                    tinv = tinv + _wdot(tinv, m)

            av = _wdot(jnp.concatenate([a_ak, a_rk], axis=0), bd(vv))
            akv, arkv = av[:C], av[C:]
            w1 = _wdot(tinv, bd(at))
            w2 = _wdot(tinv, bd(akv))
            q = rt + _wdot(a_rb, bd(w1))
            o_loc = _wdot(a_rb, bd(w2)) + arkv
            gh = _wdot_tn(jnp.concatenate([bh, kh], axis=0),
                          jnp.concatenate(
                              [jnp.concatenate([w1, w2], axis=1),
                               jnp.concatenate([jnp.zeros_like(vv), vv], axis=1)], axis=0))
            pcs = jnp.concatenate([pc[sl]] * HEADS_PER_GROUP, axis=0)
            g_bd = jnp.where(bd_mask, gh[:, :GROUP], 0.0) + jnp.where(eye_bd, pcs, 0.0)
            h_bd = jnp.where(bd_mask, gh[:, GROUP:], 0.0)

            s_old = s_ref[gi]
            qs = _wdot(jnp.concatenate([q, g_bd], axis=0), s_old)
            o_cols.append(qs[:C] + o_loc)
            s_ref[gi] = qs[C:] + h_bd
        o_rows.append(jnp.concatenate(o_cols, axis=1))
    o = jnp.concatenate(o_rows, axis=0)

    inv_n = 1.0 / HEAD
    mean = _mm_hilo(o, seg) * inv_n
    d = o - mean
    var = _mm_hilo(d * d, seg) * inv_n
    on = d * lax.rsqrt(var + GN_EPS) * gnw_ref[...] + gnb_ref[...]
    bonus = _mm_hilo(r * k2 * rk_ref[...], seg) * v
    y_rwkv = (on + bonus) * g

    y = jnp.concatenate([y_conv, y_rwkv], axis=1)
    o_ref[0] = x + _mm(y, wout_ref[...])


def _ffn_kernel(x_ref, p_ref, gffn_ref, wup_ref, cwt_ref, cb_ref, wdn_ref, wproj_ref,
                gple_ref, ggate_ref, wgate_ref, gfin_ref, o_ref, ubuf, *, ts, dff, final):
    @pl.when(pl.program_id(1) == 0)
    def _():
        ubuf[0:HALO, :] = jnp.zeros((HALO, ubuf.shape[1]), F32)

    x = x_ref[0]
    h = _rms(x, gffn_ref[...]).astype(BF16)
    acc = jnp.zeros_like(x)
    for j in range(dff // FF_CHUNK):
        outs = []
        for half in range(2):
            lo = half * dff + j * FF_CHUNK
            cols = slice(lo, lo + FF_CHUNK)
            u = jnp.dot(h, wup_ref[:, cols], preferred_element_type=F32)
            ubuf[HALO:HALO + ts, cols] = u
            uc = (u * cwt_ref[2:3, cols]
                  + ubuf[HALO - 1:HALO - 1 + ts, cols] * cwt_ref[1:2, cols]
                  + ubuf[HALO - 2:HALO - 2 + ts, cols] * cwt_ref[0:1, cols]
                  + cb_ref[:, cols])
            ubuf[0:HALO, cols] = ubuf[ts:ts + HALO, cols]
            outs.append(uc)
        gate, val = outs
        act = gate * _sigmoid(gate) * val
        acc = acc + _mm(act, wdn_ref[j * FF_CHUNK:(j + 1) * FF_CHUNK, :])
    x2 = x + acc

    e = _rms(_mm(p_ref[0], wproj_ref[...]), gple_ref[...])
    gate = _sigmoid(_mm(_rms(x2, ggate_ref[...]), wgate_ref[...]))
    x3 = x2 + gate * e
    o_ref[0] = _rms(x3, gfin_ref[...]) if final else x3


def _const_spec(shape):
    nd = len(shape)
    return pl.BlockSpec(shape, lambda b, s: (0,) * nd, pipeline_mode=pl.Buffered(1))


def _row(vec):
    return vec.reshape(1, -1).astype(F32)


def kernel(x, p, mix_norm_g, w_in, conv_mix_w, rwkv_mu, rwkv_w0, rwkv_w_up, rwkv_a0, rwkv_a_up, rwkv_g_up, rwkv_k_k, rwkv_k_a, rwkv_r_k, rwkv_gn_w, rwkv_gn_b, w_out, ffn_norm_g, ffn_w_up, ffn_conv_w, ffn_conv_b, ffn_w_down, ple_w_proj, ple_norm_g, ple_gate_norm_g, ple_w_gate, final_norm_g):
    depth = w_in.shape[0]
    bsz, seq, d = x.shape
    cw = conv_mix_w.shape[-1]
    rw = rwkv_w0.shape[-1]
    dff = ffn_w_down.shape[1]
    n_dec, n_aicl, n_gate = rwkv_w_up.shape[1], rwkv_a_up.shape[1], rwkv_g_up.shape[1]
    ts = SEQ_TILE
    assert seq % ts == 0 and ts % WKV_CHUNK == 0 and rw % GROUP == 0 and dff % FF_CHUNK == 0
    assert n_dec + n_aicl == 128 and n_gate == 128 and rwkv_r_k.shape[-1] == HEAD
    grid = (bsz, seq // ts)
    act_spec = pl.BlockSpec((1, ts, d), lambda b, s: (b, s, 0))
    params = pltpu.CompilerParams(dimension_semantics=("arbitrary", "arbitrary"),
                                  vmem_limit_bytes=VMEM_LIMIT_BYTES)

    for i in range(depth):
        wup_pad = jnp.concatenate([rwkv_w_up[i], jnp.zeros((n_aicl, rw), F32)], axis=0)
        aup_pad = jnp.concatenate([jnp.zeros((n_dec, rw), F32), rwkv_a_up[i]], axis=0)
        mixer_args = (
            _row(mix_norm_g[i]), w_in[i].astype(BF16), conv_mix_w[i].astype(F32),
            _row(rwkv_mu[i]), _row(rwkv_w0[i]), wup_pad.astype(BF16), _row(rwkv_a0[i]),
            aup_pad.astype(BF16), rwkv_g_up[i].astype(BF16), _row(rwkv_k_k[i]),
            _row(rwkv_k_a[i]), _row(rwkv_r_k[i]), _row(rwkv_gn_w[i]), _row(rwkv_gn_b[i]),
            w_out[i].astype(BF16))
        x = pl.pallas_call(
            functools.partial(_mixer_kernel, ts=ts, cw=cw, rw=rw),
            grid=grid,
            in_specs=[act_spec] + [_const_spec(a.shape) for a in mixer_args],
            out_specs=act_spec,
            out_shape=jax.ShapeDtypeStruct(x.shape, F32),
            scratch_shapes=[
                pltpu.VMEM((ts + HALO, w_in.shape[-1] - 3 * cw), F32),
                pltpu.VMEM((ts + HALO, cw), F32),
                pltpu.VMEM((rw // GROUP, GROUP, GROUP), F32),
            ],
            compiler_params=params,
            name="mixer",
        )(x, *mixer_args)

        ffn_args = (
            _row(ffn_norm_g[i]), ffn_w_up[i].astype(BF16), ffn_conv_w[i].astype(F32),
            _row(ffn_conv_b[i]), ffn_w_down[i].astype(BF16), ple_w_proj[i].astype(BF16),
            _row(ple_norm_g[i]), _row(ple_gate_norm_g[i]), ple_w_gate[i].astype(BF16))
        fin_g = _row(final_norm_g)
        x = pl.pallas_call(
            functools.partial(_ffn_kernel, ts=ts, dff=dff, final=(i == depth - 1)),
            grid=grid,
            in_specs=[act_spec, pl.BlockSpec((1, ts, p.shape[-1]), lambda b, s: (b, s, 0))]
            + [_const_spec(a.shape) for a in ffn_args] + [_const_spec(fin_g.shape)],
            out_specs=act_spec,
            out_shape=jax.ShapeDtypeStruct(x.shape, F32),
            scratch_shapes=[pltpu.VMEM((ts + HALO, 2 * dff), F32)],
            compiler_params=params,
            name="ffn",
        )(x, p[i], *ffn_args, fin_g)
    return x
```

```python
import functools
import math

import jax
import jax.numpy as jnp
from jax import lax
from jax.experimental import pallas as pl
from jax.experimental.pallas import tpu as pltpu

NORM_EPS = 1e-6
GN_EPS = 64e-5
HEAD = 64
WKV_CHUNK = 64
GROUP = 256
HEADS_PER_GROUP = GROUP // HEAD
HALO = 8
SEQ_TILE = 256
FF_CHUNK = 256
VMEM_LIMIT_BYTES = 56 * 1024 * 1024

F32 = jnp.float32
BF16 = jnp.bfloat16


def _rms(x, g):
    ms = jnp.mean(x * x, axis=-1, keepdims=True)
    return x * lax.rsqrt(ms + NORM_EPS) * g


def _sigmoid(x):
    return 1.0 / (1.0 + jnp.exp(-x))


def _mm(a, b):
    return jnp.dot(a.astype(BF16), b.astype(BF16), preferred_element_type=F32)


def _mm_hilo(a, b_bf16):
    hi = a.astype(BF16)
    lo = (a - hi.astype(F32)).astype(BF16)
    return (jnp.dot(hi, b_bf16, preferred_element_type=F32)
            + jnp.dot(lo, b_bf16, preferred_element_type=F32))


def _mm_split3(a_bf16, b):
    b1 = b.astype(BF16)
    rem = b - b1.astype(F32)
    b2 = rem.astype(BF16)
    b3 = (rem - b2.astype(F32)).astype(BF16)
    return (jnp.dot(a_bf16, b1, preferred_element_type=F32)
            + jnp.dot(a_bf16, b2, preferred_element_type=F32)
            + jnp.dot(a_bf16, b3, preferred_element_type=F32))


def _wdot(a, b):
    return jnp.dot(a.astype(BF16), b.astype(BF16), preferred_element_type=F32)


def _wdot_nt(a, b):
    return lax.dot_general(a.astype(BF16), b.astype(BF16), (((1,), (1,)), ((), ())),
                           preferred_element_type=F32)


def _wdot_tn(a, b):
    return lax.dot_general(a.astype(BF16), b.astype(BF16), (((0,), (0,)), ((), ())),
                           preferred_element_type=F32)


def _iota2(shape, dim):
    return lax.broadcasted_iota(jnp.int32, shape, dim)


def _mixer_kernel(x_ref, g_ref, win_ref, cw_ref, mu_ref, w0_ref, wup_ref, a0_ref, aup_ref,
                  gup_ref, kk_ref, ka_ref, rk_ref, gnw_ref, gnb_ref, wout_ref,
                  o_ref, zbuf, cxbuf, s_ref, *, ts, cw, rw):
    n_chunks = ts // WKV_CHUNK
    n_groups = rw // GROUP
    C = WKV_CHUNK

    @pl.when(pl.program_id(1) == 0)
    def _():
        zbuf[0:HALO, :] = jnp.zeros((HALO, zbuf.shape[1]), F32)
        cxbuf[0:HALO, :] = jnp.zeros((HALO, cxbuf.shape[1]), F32)
        s_ref[...] = jnp.zeros(s_ref.shape, F32)

    x = x_ref[0]
    h = _rms(x, g_ref[...])
    z = _mm(h, win_ref[...])

    x_in = z[:, 0:cw]
    b_gate = z[:, cw:2 * cw]
    c_gate = z[:, 2 * cw:3 * cw]
    cx = c_gate * x_in
    cxbuf[HALO:HALO + ts, :] = cx
    conv = (cx * cw_ref[2:3, :]
            + cxbuf[HALO - 1:HALO - 1 + ts, :] * cw_ref[1:2, :]
            + cxbuf[HALO - 2:HALO - 2 + ts, :] * cw_ref[0:1, :])
    y_conv = b_gate * conv
    cxbuf[0:HALO, :] = cxbuf[ts:ts + HALO, :]

    zr = z[:, 3 * cw:]
    zbuf[HALO:HALO + ts, :] = zr
    zs = zbuf[HALO - 1:HALO - 1 + ts, :]
    zbuf[0:HALO, :] = zbuf[ts:ts + HALO, :]
    zz = zr + (zs - zr) * mu_ref[...]
    r = zz[:, 0:rw]
    k = zz[:, rw:2 * rw]
    v = zz[:, 2 * rw:3 * rw]
    lora_in = zz[:, 3 * rw:3 * rw + 128]
    gd = zz[:, 3 * rw + 128:3 * rw + 256]

    u_dec = w0_ref[...] + _mm(jnp.tanh(lora_in), wup_ref[...])
    logw = (-math.exp(-0.5)) * _sigmoid(u_dec)
    a = _sigmoid(a0_ref[...] + _mm(lora_in, aup_ref[...]))
    g = _mm(_sigmoid(gd), gup_ref[...])

    seg = (_iota2((rw, rw), 0) // HEAD == _iota2((rw, rw), 1) // HEAD).astype(BF16)

    kk = k * kk_ref[...]
    kk = kk * lax.rsqrt(jnp.maximum(_mm_hilo(kk * kk, seg), 1e-24))
    k2 = k * (1.0 + (a - 1.0) * ka_ref[...])
    aa = -kk
    bb = kk * a

    ri = _iota2((ts, ts), 0)
    ci = _iota2((ts, ts), 1)
    same_chunk = (ri // C) == (ci // C)
    tri = jnp.where(same_chunk & (ci <= ri), 1.0, 0.0).astype(BF16)
    ones_blk = jnp.where(same_chunk, 1.0, 0.0).astype(BF16)
    csum = _mm_split3(jnp.concatenate([tri, ones_blk], axis=0), logw)
    cl = csum[:ts]
    ctot = csum[ts:]
    ec = jnp.exp(cl)
    eci = jnp.exp(-cl)
    ecm = jnp.exp(cl - logw)
    ecl = jnp.exp(ctot - cl)
    pc = jnp.exp(ctot)
    a_t = aa * ecm
    r_t = r * ec
    b_t = bb * eci
    k_t = k2 * eci
    b_h = bb * ecl
    k_h = k2 * ecl

    rs = _iota2((C, GROUP), 0)
    cs = _iota2((C, GROUP), 1) % HEAD
    lower_strict = cs < rs
    lower_incl = cs <= rs
    eye_sbs = jnp.where(cs == rs, 1.0, 0.0).astype(F32)
    rb = _iota2((GROUP, GROUP), 0)
    cb = _iota2((GROUP, GROUP), 1)
    bd_mask = (rb // HEAD) == (cb // HEAD)
    eye_bd = rb == cb

    def bd(y):
        yb = y.astype(BF16)
        return jnp.where(bd_mask, jnp.concatenate([yb] * HEADS_PER_GROUP, axis=0),
                         jnp.zeros((), BF16))

    o_rows = []
    for c in range(n_chunks):
        o_cols = []
        for gi in range(n_groups):
            sl = (slice(c * C, (c + 1) * C), slice(gi * GROUP, (gi + 1) * GROUP))
            at, rt, bt, kt, bh, kh, vv = (a_t[sl], r_t[sl], b_t[sl], k_t[sl], b_h[sl],
                                          k_h[sl], v[sl])
            ar = jnp.concatenate([at, rt], axis=0)
            a1 = _wdot_nt(ar, bd(bt))
            a2 = _wdot_nt(ar, bd(kt))
            a_ab = jnp.where(lower_strict, a1[:C], 0.0)
            a_rb = jnp.where(lower_incl, a1[C:], 0.0)
            a_ak = jnp.where(lower_strict, a2[:C], 0.0)
            a_rk = jnp.where(lower_incl, a2[C:], 0.0)

            lp = a_ab
            tinv = eye_sbs + a_ab
            n_sq = int(math.log2(C))
            for i in range(n_sq):
                m = bd(lp)
                if i == 0:
                    lp = _wdot(lp, m)
                elif i < n_sq - 1:
                    st = _wdot(jnp.concatenate([lp, tinv], axis=0), m)
                    lp = st[:C]
                    tinv = tinv + st[C:]
                else:
                    tinv = tinv + _wdot(tinv, m)

            av = _wdot(jnp.concatenate([a_ak, a_rk], axis=0), bd(vv))
            akv, arkv = av[:C], av[C:]
            w1 = _wdot(tinv, bd(at))
            w2 = _wdot(tinv, bd(akv))
            q = rt + _wdot(a_rb, bd(w1))
            o_loc = _wdot(a_rb, bd(w2)) + arkv
            gh = _wdot_tn(jnp.concatenate([bh, kh], axis=0),
                          jnp.concatenate(
                              [jnp.concatenate([w1, w2], axis=1),
                               jnp.concatenate([jnp.zeros_like(vv), vv], axis=1)], axis=0))
            pcs = jnp.concatenate([pc[sl]] * HEADS_PER_GROUP, axis=0)
            g_bd = jnp.where(bd_mask, gh[:, :GROUP], 0.0) + jnp.where(eye_bd, pcs, 0.0)
            h_bd = jnp.where(bd_mask, gh[:, GROUP:], 0.0)

            s_old = s_ref[gi]
            qs = _wdot(jnp.concatenate([q, g_bd], axis=0), s_old)
            o_cols.append(qs[:C] + o_loc)
            s_ref[gi] = qs[C:] + h_bd
        o_rows.append(jnp.concatenate(o_cols, axis=1))
    o = jnp.concatenate(o_rows, axis=0)

    inv_n = 1.0 / HEAD
    mean = _mm_hilo(o, seg) * inv_n
    d = o - mean
    var = _mm_hilo(d * d, seg) * inv_n
    on = d * lax.rsqrt(var + GN_EPS) * gnw_ref[...] + gnb_ref[...]
    bonus = _mm_hilo(r * k2 * rk_ref[...], seg) * v
    y_rwkv = (on + bonus) * g

    y = jnp.concatenate([y_conv, y_rwkv], axis=1)
    o_ref[0] = x + _mm(y, wout_ref[...])


def _ffn_kernel(x_ref, p_ref, gffn_ref, wup_ref, cwt_ref, cb_ref, wdn_ref, wproj_ref,
                gple_ref, ggate_ref, wgate_ref, gfin_ref, o_ref, ubuf, *, ts, dff, final):
    @pl.when(pl.program_id(1) == 0)
    def _():
        ubuf[0:HALO, :] = jnp.zeros((HALO, ubuf.shape[1]), F32)

    x = x_ref[0]
    h = _rms(x, gffn_ref[...]).astype(BF16)
    acc = jnp.zeros_like(x)
    for j in range(dff // FF_CHUNK):
        outs = []
        for half in range(2):
            lo = half * dff + j * FF_CHUNK
            cols = slice(lo, lo + FF_CHUNK)
            u = jnp.dot(h, wup_ref[:, cols], preferred_element_type=F32)
            ubuf[HALO:HALO + ts, cols] = u
            uc = (u * cwt_ref[2:3, cols]
                  + ubuf[HALO - 1:HALO - 1 + ts, cols] * cwt_ref[1:2, cols]
                  + ubuf[HALO - 2:HALO - 2 + ts, cols] * cwt_ref[0:1, cols]
                  + cb_ref[:, cols])
            ubuf[0:HALO, cols] = ubuf[ts:ts + HALO, cols]
            outs.append(uc)
        gate, val = outs
        act = gate * _sigmoid(gate) * val
        acc = acc + _mm(act, wdn_ref[j * FF_CHUNK:(j + 1) * FF_CHUNK, :])
    x2 = x + acc

    e = _rms(_mm(p_ref[0], wproj_ref[...]), gple_ref[...])
    gate = _sigmoid(_mm(_rms(x2, ggate_ref[...]), wgate_ref[...]))
    x3 = x2 + gate * e
    o_ref[0] = _rms(x3, gfin_ref[...]) if final else x3


def _const_spec(shape):
    nd = len(shape)
    return pl.BlockSpec(shape, lambda b, s: (0,) * nd, pipeline_mode=pl.Buffered(1))


def _row(vec):
    return vec.reshape(1, -1).astype(F32)


def kernel(x, p, mix_norm_g, w_in, conv_mix_w, rwkv_mu, rwkv_w0, rwkv_w_up, rwkv_a0, rwkv_a_up, rwkv_g_up, rwkv_k_k, rwkv_k_a, rwkv_r_k, rwkv_gn_w, rwkv_gn_b, w_out, ffn_norm_g, ffn_w_up, ffn_conv_w, ffn_conv_b, ffn_w_down, ple_w_proj, ple_norm_g, ple_gate_norm_g, ple_w_gate, final_norm_g):
    depth = w_in.shape[0]
    bsz, seq, d = x.shape
    cw = conv_mix_w.shape[-1]
    rw = rwkv_w0.shape[-1]
    dff = ffn_w_down.shape[1]
    n_dec, n_aicl, n_gate = rwkv_w_up.shape[1], rwkv_a_up.shape[1], rwkv_g_up.shape[1]
    ts = SEQ_TILE
    assert seq % ts == 0 and ts % WKV_CHUNK == 0 and rw % GROUP == 0 and dff % FF_CHUNK == 0
    assert n_dec + n_aicl == 128 and n_gate == 128 and rwkv_r_k.shape[-1] == HEAD
    grid = (bsz, seq // ts)
    act_spec = pl.BlockSpec((1, ts, d), lambda b, s: (b, s, 0))
    params = pltpu.CompilerParams(dimension_semantics=("arbitrary", "arbitrary"),
                                  vmem_limit_bytes=VMEM_LIMIT_BYTES)

    for i in range(depth):
        wup_pad = jnp.concatenate([rwkv_w_up[i], jnp.zeros((n_aicl, rw), F32)], axis=0)
        aup_pad = jnp.concatenate([jnp.zeros((n_dec, rw), F32), rwkv_a_up[i]], axis=0)
        mixer_args = (
            _row(mix_norm_g[i]), w_in[i].astype(BF16), conv_mix_w[i].astype(F32),
            _row(rwkv_mu[i]), _row(rwkv_w0[i]), wup_pad.astype(BF16), _row(rwkv_a0[i]),
            aup_pad.astype(BF16), rwkv_g_up[i].astype(BF16), _row(rwkv_k_k[i]),
            _row(rwkv_k_a[i]), _row(rwkv_r_k[i]), _row(rwkv_gn_w[i]), _row(rwkv_gn_b[i]),
            w_out[i].astype(BF16))
        x = pl.pallas_call(
            functools.partial(_mixer_kernel, ts=ts, cw=cw, rw=rw),
            grid=grid,
            in_specs=[act_spec] + [_const_spec(a.shape) for a in mixer_args],
            out_specs=act_spec,
            out_shape=jax.ShapeDtypeStruct(x.shape, F32),
            scratch_shapes=[
                pltpu.VMEM((ts + HALO, w_in.shape[-1] - 3 * cw), F32),
                pltpu.VMEM((ts + HALO, cw), F32),
                pltpu.VMEM((rw // GROUP, GROUP, GROUP), F32),
            ],
            compiler_params=params,
            name="mixer",
        )(x, *mixer_args)

        ffn_args = (
            _row(ffn_norm_g[i]), ffn_w_up[i].astype(BF16), ffn_conv_w[i].astype(F32),
            _row(ffn_conv_b[i]), ffn_w_down[i].astype(BF16), ple_w_proj[i].astype(BF16),
            _row(ple_norm_g[i]), _row(ple_gate_norm_g[i]), ple_w_gate[i].astype(BF16))
        fin_g = _row(final_norm_g)
        x = pl.pallas_call(
            functools.partial(_ffn_kernel, ts=ts, dff=dff, final=(i == depth - 1)),
            grid=grid,
            in_specs=[act_spec, pl.BlockSpec((1, ts, p.shape[-1]), lambda b, s: (b, s, 0))]
            + [_const_spec(a.shape) for a in ffn_args] + [_const_spec(fin_g.shape)],
            out_specs=act_spec,
            out_shape=jax.ShapeDtypeStruct(x.shape, F32),
            scratch_shapes=[pltpu.VMEM((ts + HALO, 2 * dff), F32)],
            compiler_params=params,
            name="ffn",
        )(x, p[i], *ffn_args, fin_g)
    return x
```

```python
import functools
import math

import jax
import jax.numpy as jnp
from jax import lax
from jax.experimental import pallas as pl
from jax.experimental.pallas import tpu as pltpu

NORM_EPS = 1e-6
GN_EPS = 64e-5
HEAD = 64
WKV_CHUNK = 64
GROUP = 256
HEADS_PER_GROUP = GROUP // HEAD
HALO = 8
SEQ_TILE = 256
FF_CHUNK = 256
VMEM_LIMIT_BYTES = 56 * 1024 * 1024

F32 = jnp.float32
BF16 = jnp.bfloat16


def _rms(x, g):
    ms = jnp.mean(x * x, axis=-1, keepdims=True)
    return x * lax.rsqrt(ms + NORM_EPS) * g


def _sigmoid(x):
    return 1.0 / (1.0 + jnp.exp(-x))


def _mm(a, b):
    return jnp.dot(a.astype(BF16), b.astype(BF16), preferred_element_type=F32)


def _mm_hilo(a, b_bf16):
    hi = a.astype(BF16)
    lo = (a - hi.astype(F32)).astype(BF16)
    return (jnp.dot(hi, b_bf16, preferred_element_type=F32)
            + jnp.dot(lo, b_bf16, preferred_element_type=F32))


def _mm_split3(a_bf16, b):
    b1 = b.astype(BF16)
    rem = b - b1.astype(F32)
    b2 = rem.astype(BF16)
    b3 = (rem - b2.astype(F32)).astype(BF16)
    return (jnp.dot(a_bf16, b1, preferred_element_type=F32)
            + jnp.dot(a_bf16, b2, preferred_element_type=F32)
            + jnp.dot(a_bf16, b3, preferred_element_type=F32))


def _wdot(a, b):
    return jnp.dot(a.astype(BF16), b.astype(BF16), preferred_element_type=F32)


def _wdot_nt(a, b):
    return lax.dot_general(a.astype(BF16), b.astype(BF16), (((1,), (1,)), ((), ())),
                           preferred_element_type=F32)


def _wdot_tn(a, b):
    return lax.dot_general(a.astype(BF16), b.astype(BF16), (((0,), (0,)), ((), ())),
                           preferred_element_type=F32)


def _iota2(shape, dim):
    return lax.broadcasted_iota(jnp.int32, shape, dim)


def _mixer_kernel(x_ref, g_ref, win_ref, cw_ref, mu_ref, w0_ref, wup_ref, a0_ref, aup_ref,
                  gup_ref, kk_ref, ka_ref, rk_ref, gnw_ref, gnb_ref, wout_ref,
                  o_ref, zbuf, cxbuf, s_ref, *, ts, cw, rw):
    n_chunks = ts // WKV_CHUNK
    n_groups = rw // GROUP
    C = WKV_CHUNK

    @pl.when(pl.program_id(1) == 0)
    def _():
        zbuf[0:HALO, :] = jnp.zeros((HALO, zbuf.shape[1]), F32)
        cxbuf[0:HALO, :] = jnp.zeros((HALO, cxbuf.shape[1]), F32)
        s_ref[...] = jnp.zeros(s_ref.shape, F32)

    x = x_ref[0]
    h = _rms(x, g_ref[...])
    z = _mm(h, win_ref[...])

    x_in = z[:, 0:cw]
    b_gate = z[:, cw:2 * cw]
    c_gate = z[:, 2 * cw:3 * cw]
    cx = c_gate * x_in
    cxbuf[HALO:HALO + ts, :] = cx
    conv = (cx * cw_ref[2:3, :]
            + cxbuf[HALO - 1:HALO - 1 + ts, :] * cw_ref[1:2, :]
            + cxbuf[HALO - 2:HALO - 2 + ts, :] * cw_ref[0:1, :])
    y_conv = b_gate * conv
    cxbuf[0:HALO, :] = cxbuf[ts:ts + HALO, :]

    zr = z[:, 3 * cw:]
    zbuf[HALO:HALO + ts, :] = zr
    zs = zbuf[HALO - 1:HALO - 1 + ts, :]
    zbuf[0:HALO, :] = zbuf[ts:ts + HALO, :]
    zz = zr + (zs - zr) * mu_ref[...]
    r = zz[:, 0:rw]
    k = zz[:, rw:2 * rw]
    v = zz[:, 2 * rw:3 * rw]
    lora_in = zz[:, 3 * rw:3 * rw + 128]
    gd = zz[:, 3 * rw + 128:3 * rw + 256]

    u_dec = w0_ref[...] + _mm(jnp.tanh(lora_in), wup_ref[...])
    logw = (-math.exp(-0.5)) * _sigmoid(u_dec)
    a = _sigmoid(a0_ref[...] + _mm(lora_in, aup_ref[...]))
    g = _mm(_sigmoid(gd), gup_ref[...])

    seg = (_iota2((rw, rw), 0) // HEAD == _iota2((rw, rw), 1) // HEAD).astype(BF16)

    kk = k * kk_ref[...]
    kk = kk * lax.rsqrt(jnp.maximum(_mm_hilo(kk * kk, seg), 1e-24))
    k2 = k * (1.0 + (a - 1.0) * ka_ref[...])
    aa = -kk
    bb = kk * a

    ri = _iota2((ts, ts), 0)
    ci = _iota2((ts, ts), 1)
    same_chunk = (ri // C) == (ci // C)
    tri = jnp.where(same_chunk & (ci <= ri), 1.0, 0.0).astype(BF16)
    ones_blk = jnp.where(same_chunk, 1.0, 0.0).astype(BF16)
    csum = _mm_split3(jnp.concatenate([tri, ones_blk], axis=0), logw)
    cl = csum[:ts]
    ctot = csum[ts:]
    ec = jnp.exp(cl)
    eci = jnp.exp(-cl)
    ecm = jnp.exp(cl - logw)
    ecl = jnp.exp(ctot - cl)
    pc = jnp.exp(ctot)
    a_t = aa * ecm
    r_t = r * ec
    b_t = bb * eci
    k_t = k2 * eci
    b_h = bb * ecl
    k_h = k2 * ecl

    rs = _iota2((C, GROUP), 0)
    cs = _iota2((C, GROUP), 1) % HEAD
    lower_strict = cs < rs
    lower_incl = cs <= rs
    eye_sbs = jnp.where(cs == rs, 1.0, 0.0).astype(F32)
    rb = _iota2((GROUP, GROUP), 0)
    cb = _iota2((GROUP, GROUP), 1)
    bd_mask = (rb // HEAD) == (cb // HEAD)
    eye_bd = rb == cb

    def bd(y):
        yb = y.astype(BF16)
        return jnp.where(bd_mask, jnp.concatenate([yb] * HEADS_PER_GROUP, axis=0),
                         jnp.zeros((), BF16))

    units = [(c, gi) for c in range(n_chunks) for gi in range(n_groups)]
    nu = len(units)

    def cut(arr):
        return [arr[c * C:(c + 1) * C, gi * GROUP:(gi + 1) * GROUP] for c, gi in units]

    at, rt, bt, kt, bh, kh, vv, pcu = (cut(a_t), cut(r_t), cut(b_t), cut(k_t), cut(b_h),
                                       cut(k_h), cut(v), cut(pc))
    ar = [jnp.concatenate([at[u], rt[u]], axis=0) for u in range(nu)]
    a1 = [_wdot_nt(ar[u], bd(bt[u])) for u in range(nu)]
    a2 = [_wdot_nt(ar[u], bd(kt[u])) for u in range(nu)]
    a_ab = [jnp.where(lower_strict, a1[u][:C], 0.0) for u in range(nu)]
    a_rb = [jnp.where(lower_incl, a1[u][C:], 0.0) for u in range(nu)]
    a_ak = [jnp.where(lower_strict, a2[u][:C], 0.0) for u in range(nu)]
    a_rk = [jnp.where(lower_incl, a2[u][C:], 0.0) for u in range(nu)]
    av = [_wdot(jnp.concatenate([a_ak[u], a_rk[u]], axis=0), bd(vv[u])) for u in range(nu)]

    lp = a_ab
    tinv = [eye_sbs + a_ab[u] for u in range(nu)]
    n_sq = int(math.log2(C))
    for i in range(n_sq):
        m = [bd(lp[u]) for u in range(nu)]
        if i == 0:
            lp = [_wdot(lp[u], m[u]) for u in range(nu)]
        elif i < n_sq - 1:
            st = [_wdot(jnp.concatenate([lp[u], tinv[u]], axis=0), m[u]) for u in range(nu)]
            lp = [st[u][:C] for u in range(nu)]
            tinv = [tinv[u] + st[u][C:] for u in range(nu)]
        else:
            tinv = [tinv[u] + _wdot(tinv[u], m[u]) for u in range(nu)]

    w1 = [_wdot(tinv[u], bd(at[u])) for u in range(nu)]
    w2 = [_wdot(tinv[u], bd(av[u][:C])) for u in range(nu)]
    q = [rt[u] + _wdot(a_rb[u], bd(w1[u])) for u in range(nu)]
    o_loc = [_wdot(a_rb[u], bd(w2[u])) + av[u][C:] for u in range(nu)]
    gh = [_wdot_tn(jnp.concatenate([bh[u], kh[u]], axis=0),
                   jnp.concatenate(
                       [jnp.concatenate([w1[u], w2[u]], axis=1),
                        jnp.concatenate([jnp.zeros_like(vv[u]), vv[u]], axis=1)], axis=0))
          for u in range(nu)]
    g_bd = [jnp.where(bd_mask, gh[u][:, :GROUP], 0.0)
            + jnp.where(eye_bd, jnp.concatenate([pcu[u]] * HEADS_PER_GROUP, axis=0), 0.0)
            for u in range(nu)]
    h_bd = [jnp.where(bd_mask, gh[u][:, GROUP:], 0.0) for u in range(nu)]

    state = [s_ref[gi] for gi in range(n_groups)]
    o_rows = []
    for c in range(n_chunks):
        o_cols = []
        for gi in range(n_groups):
            u = c * n_groups + gi
            qs = _wdot(jnp.concatenate([q[u], g_bd[u]], axis=0), state[gi])
            o_cols.append(qs[:C] + o_loc[u])
            state[gi] = qs[C:] + h_bd[u]
        o_rows.append(jnp.concatenate(o_cols, axis=1))
    for gi in range(n_groups):
        s_ref[gi] = state[gi]
    o = jnp.concatenate(o_rows, axis=0)

    inv_n = 1.0 / HEAD
    mean = _mm_hilo(o, seg) * inv_n
    d = o - mean
    var = _mm_hilo(d * d, seg) * inv_n
    on = d * lax.rsqrt(var + GN_EPS) * gnw_ref[...] + gnb_ref[...]
    bonus = _mm_hilo(r * k2 * rk_ref[...], seg) * v
    y_rwkv = (on + bonus) * g

    y = jnp.concatenate([y_conv, y_rwkv], axis=1)
    o_ref[0] = x + _mm(y, wout_ref[...])


def _ffn_kernel(x_ref, p_ref, gffn_ref, wup_ref, cwt_ref, cb_ref, wdn_ref, wproj_ref,
                gple_ref, ggate_ref, wgate_ref, gfin_ref, o_ref, ubuf, *, ts, dff, final):
    @pl.when(pl.program_id(1) == 0)
    def _():
        ubuf[0:HALO, :] = jnp.zeros((HALO, ubuf.shape[1]), F32)

    x = x_ref[0]
    h = _rms(x, gffn_ref[...]).astype(BF16)
    acc = jnp.zeros_like(x)
    for j in range(dff // FF_CHUNK):
        outs = []
        for half in range(2):
            lo = half * dff + j * FF_CHUNK
            cols = slice(lo, lo + FF_CHUNK)
            u = jnp.dot(h, wup_ref[:, cols], preferred_element_type=F32)
            ubuf[HALO:HALO + ts, cols] = u
            uc = (u * cwt_ref[2:3, cols]
                  + ubuf[HALO - 1:HALO - 1 + ts, cols] * cwt_ref[1:2, cols]
                  + ubuf[HALO - 2:HALO - 2 + ts, cols] * cwt_ref[0:1, cols]
                  + cb_ref[:, cols])
            ubuf[0:HALO, cols] = ubuf[ts:ts + HALO, cols]
            outs.append(uc)
        gate, val = outs
        act = gate * _sigmoid(gate) * val
        acc = acc + _mm(act, wdn_ref[j * FF_CHUNK:(j + 1) * FF_CHUNK, :])
    x2 = x + acc

    e = _rms(_mm(p_ref[0], wproj_ref[...]), gple_ref[...])
    gate = _sigmoid(_mm(_rms(x2, ggate_ref[...]), wgate_ref[...]))
    x3 = x2 + gate * e
    o_ref[0] = _rms(x3, gfin_ref[...]) if final else x3


def _const_spec(shape):
    nd = len(shape)
    return pl.BlockSpec(shape, lambda b, s: (0,) * nd, pipeline_mode=pl.Buffered(1))


def _row(vec):
    return vec.reshape(1, -1).astype(F32)


def kernel(x, p, mix_norm_g, w_in, conv_mix_w, rwkv_mu, rwkv_w0, rwkv_w_up, rwkv_a0, rwkv_a_up, rwkv_g_up, rwkv_k_k, rwkv_k_a, rwkv_r_k, rwkv_gn_w, rwkv_gn_b, w_out, ffn_norm_g, ffn_w_up, ffn_conv_w, ffn_conv_b, ffn_w_down, ple_w_proj, ple_norm_g, ple_gate_norm_g, ple_w_gate, final_norm_g):
    depth = w_in.shape[0]
    bsz, seq, d = x.shape
    cw = conv_mix_w.shape[-1]
    rw = rwkv_w0.shape[-1]
    dff = ffn_w_down.shape[1]
    n_dec, n_aicl, n_gate = rwkv_w_up.shape[1], rwkv_a_up.shape[1], rwkv_g_up.shape[1]
    ts = SEQ_TILE
    assert seq % ts == 0 and ts % WKV_CHUNK == 0 and rw % GROUP == 0 and dff % FF_CHUNK == 0
    assert n_dec + n_aicl == 128 and n_gate == 128 and rwkv_r_k.shape[-1] == HEAD
    grid = (bsz, seq // ts)
    act_spec = pl.BlockSpec((1, ts, d), lambda b, s: (b, s, 0))
    params = pltpu.CompilerParams(dimension_semantics=("arbitrary", "arbitrary"),
                                  vmem_limit_bytes=VMEM_LIMIT_BYTES)

    for i in range(depth):
        wup_pad = jnp.concatenate([rwkv_w_up[i], jnp.zeros((n_aicl, rw), F32)], axis=0)
        aup_pad = jnp.concatenate([jnp.zeros((n_dec, rw), F32), rwkv_a_up[i]], axis=0)
        mixer_args = (
            _row(mix_norm_g[i]), w_in[i].astype(BF16), conv_mix_w[i].astype(F32),
            _row(rwkv_mu[i]), _row(rwkv_w0[i]), wup_pad.astype(BF16), _row(rwkv_a0[i]),
            aup_pad.astype(BF16), rwkv_g_up[i].astype(BF16), _row(rwkv_k_k[i]),
            _row(rwkv_k_a[i]), _row(rwkv_r_k[i]), _row(rwkv_gn_w[i]), _row(rwkv_gn_b[i]),
            w_out[i].astype(BF16))
        x = pl.pallas_call(
            functools.partial(_mixer_kernel, ts=ts, cw=cw, rw=rw),
            grid=grid,
            in_specs=[act_spec] + [_const_spec(a.shape) for a in mixer_args],
            out_specs=act_spec,
            out_shape=jax.ShapeDtypeStruct(x.shape, F32),
            scratch_shapes=[
                pltpu.VMEM((ts + HALO, w_in.shape[-1] - 3 * cw), F32),
                pltpu.VMEM((ts + HALO, cw), F32),
                pltpu.VMEM((rw // GROUP, GROUP, GROUP), F32),
            ],
            compiler_params=params,
            name="mixer",
        )(x, *mixer_args)

        ffn_args = (
            _row(ffn_norm_g[i]), ffn_w_up[i].astype(BF16), ffn_conv_w[i].astype(F32),
            _row(ffn_conv_b[i]), ffn_w_down[i].astype(BF16), ple_w_proj[i].astype(BF16),
            _row(ple_norm_g[i]), _row(ple_gate_norm_g[i]), ple_w_gate[i].astype(BF16))
        fin_g = _row(final_norm_g)
        x = pl.pallas_call(
            functools.partial(_ffn_kernel, ts=ts, dff=dff, final=(i == depth - 1)),
            grid=grid,
            in_specs=[act_spec, pl.BlockSpec((1, ts, p.shape[-1]), lambda b, s: (b, s, 0))]
            + [_const_spec(a.shape) for a in ffn_args] + [_const_spec(fin_g.shape)],
            out_specs=act_spec,
            out_shape=jax.ShapeDtypeStruct(x.shape, F32),
            scratch_shapes=[pltpu.VMEM((ts + HALO, 2 * dff), F32)],
            compiler_params=params,
            name="ffn",
        )(x, p[i], *ffn_args, fin_g)
    return x
```

```python
import functools
import math

import jax
import jax.numpy as jnp
from jax import lax
from jax.experimental import pallas as pl
from jax.experimental.pallas import tpu as pltpu

NORM_EPS = 1e-6
GN_EPS = 64e-5
HEAD = 64
WKV_CHUNK = 64
GROUP = 256
HEADS_PER_GROUP = GROUP // HEAD
HALO = 8
SEQ_TILE = 256
FF_CHUNK = 256
VMEM_LIMIT_BYTES = 56 * 1024 * 1024

F32 = jnp.float32
BF16 = jnp.bfloat16


def _rms(x, g):
    ms = jnp.mean(x * x, axis=-1, keepdims=True)
    return x * lax.rsqrt(ms + NORM_EPS) * g


def _sigmoid(x):
    return 1.0 / (1.0 + jnp.exp(-x))


def _mm(a, b):
    return jnp.dot(a.astype(BF16), b.astype(BF16), preferred_element_type=F32)


def _mm_split3(a_bf16, b):
    b1 = b.astype(BF16)
    rem = b - b1.astype(F32)
    b2 = rem.astype(BF16)
    b3 = (rem - b2.astype(F32)).astype(BF16)
    return (jnp.dot(a_bf16, b1, preferred_element_type=F32)
            + jnp.dot(a_bf16, b2, preferred_element_type=F32)
            + jnp.dot(a_bf16, b3, preferred_element_type=F32))


def _wdot(a, b):
    return jnp.dot(a.astype(BF16), b.astype(BF16), preferred_element_type=F32)


def _wdot_nt(a, b):
    return lax.dot_general(a.astype(BF16), b.astype(BF16), (((1,), (1,)), ((), ())),
                           preferred_element_type=F32)


def _wdot_tn(a, b):
    return lax.dot_general(a.astype(BF16), b.astype(BF16), (((0,), (0,)), ((), ())),
                           preferred_element_type=F32)


def _iota2(shape, dim):
    return lax.broadcasted_iota(jnp.int32, shape, dim)


def _delay_rows(cur, prev_tail, n):
    rolled = pltpu.roll(cur, n, axis=0)
    head = jnp.where(_iota2((HALO, cur.shape[1]), 0) < n,
                     pltpu.roll(prev_tail, n, axis=0), rolled[:HALO])
    return jnp.concatenate([head, rolled[HALO:]], axis=0)


def _mixer_kernel(x_ref, g_ref, win_ref, cw_ref, mu_ref, w0_ref, wup_ref, a0_ref, aup_ref,
                  gup_ref, kk_ref, ka_ref, rk_ref, gnw_ref, gnb_ref, wout_ref,
                  o_ref, ztail, cxtail, s_ref, *, ts, cw, rw):
    n_chunks = ts // WKV_CHUNK
    n_groups = rw // GROUP
    C = WKV_CHUNK

    @pl.when(pl.program_id(1) == 0)
    def _():
        ztail[...] = jnp.zeros(ztail.shape, F32)
        cxtail[...] = jnp.zeros(cxtail.shape, F32)
        s_ref[...] = jnp.zeros(s_ref.shape, F32)

    x = x_ref[0]
    h = _rms(x, g_ref[...])
    z = _mm(h, win_ref[...])

    x_in = z[:, 0:cw]
    b_gate = z[:, cw:2 * cw]
    c_gate = z[:, 2 * cw:3 * cw]
    cx = c_gate * x_in
    cx_prev = cxtail[...]
    conv = (cx * cw_ref[2:3, :]
            + _delay_rows(cx, cx_prev, 1) * cw_ref[1:2, :]
            + _delay_rows(cx, cx_prev, 2) * cw_ref[0:1, :])
    y_conv = b_gate * conv
    cxtail[...] = cx[ts - HALO:, :]

    zr = z[:, 3 * cw:]
    zs = _delay_rows(zr, ztail[...], 1)
    ztail[...] = zr[ts - HALO:, :]
    zz = zr + (zs - zr) * mu_ref[...]
    r = zz[:, 0:rw]
    k = zz[:, rw:2 * rw]
    v = zz[:, 2 * rw:3 * rw]
    lora_in = zz[:, 3 * rw:3 * rw + 128]
    gd = zz[:, 3 * rw + 128:3 * rw + 256]

    u_dec = w0_ref[...] + _mm(jnp.tanh(lora_in), wup_ref[...])
    logw = (-math.exp(-0.5)) * _sigmoid(u_dec)
    a = _sigmoid(a0_ref[...] + _mm(lora_in, aup_ref[...]))
    g = _mm(_sigmoid(gd), gup_ref[...])

    seg = (_iota2((rw, rw), 0) // HEAD == _iota2((rw, rw), 1) // HEAD).astype(BF16)

    kk = k * kk_ref[...]
    kk = kk * lax.rsqrt(jnp.maximum(_mm(kk * kk, seg), 1e-24))
    k2 = k * (1.0 + (a - 1.0) * ka_ref[...])
    aa = -kk
    bb = kk * a

    ri = _iota2((ts, ts), 0)
    ci = _iota2((ts, ts), 1)
    same_chunk = (ri // C) == (ci // C)
    tri = jnp.where(same_chunk & (ci <= ri), 1.0, 0.0).astype(BF16)
    ones_blk = jnp.where(same_chunk, 1.0, 0.0).astype(BF16)
    csum = _mm_split3(jnp.concatenate([tri, ones_blk], axis=0), logw)
    cl = csum[:ts]
    ctot = csum[ts:]
    ec = jnp.exp(cl)
    eci = jnp.exp(-cl)
    ecm = jnp.exp(cl - logw)
    ecl = jnp.exp(ctot - cl)
    pc = jnp.exp(ctot)
    a_t = aa * ecm
    r_t = r * ec
    b_t = bb * eci
    k_t = k2 * eci
    b_h = bb * ecl
    k_h = k2 * ecl

    rs = _iota2((C, GROUP), 0)
    cs = _iota2((C, GROUP), 1) % HEAD
    lower_strict = cs < rs
    lower_incl = cs <= rs
    eye_sbs = jnp.where(cs == rs, 1.0, 0.0).astype(F32)
    rb = _iota2((GROUP, GROUP), 0)
    cb = _iota2((GROUP, GROUP), 1)
    bd_mask = (rb // HEAD) == (cb // HEAD)
    eye_bd = rb == cb

    def bd(y):
        yb = y.astype(BF16)
        return jnp.where(bd_mask, jnp.concatenate([yb] * HEADS_PER_GROUP, axis=0),
                         jnp.zeros((), BF16))

    units = [(c, gi) for c in range(n_chunks) for gi in range(n_groups)]
    nu = len(units)

    def cut(arr):
        return [arr[c * C:(c + 1) * C, gi * GROUP:(gi + 1) * GROUP] for c, gi in units]

    at, rt, bt, kt, bh, kh, vv, pcu = (cut(a_t), cut(r_t), cut(b_t), cut(k_t), cut(b_h),
                                       cut(k_h), cut(v), cut(pc))
    ar = [jnp.concatenate([at[u], rt[u]], axis=0) for u in range(nu)]
    a1 = [_wdot_nt(ar[u], bd(bt[u])) for u in range(nu)]
    a2 = [_wdot_nt(ar[u], bd(kt[u])) for u in range(nu)]
    a_ab = [jnp.where(lower_strict, a1[u][:C], 0.0) for u in range(nu)]
    a_rb = [jnp.where(lower_incl, a1[u][C:], 0.0) for u in range(nu)]
    a_ak = [jnp.where(lower_strict, a2[u][:C], 0.0) for u in range(nu)]
    a_rk = [jnp.where(lower_incl, a2[u][C:], 0.0) for u in range(nu)]
    av = [_wdot(jnp.concatenate([a_ak[u], a_rk[u]], axis=0), bd(vv[u])) for u in range(nu)]

    lp = a_ab
    tinv = [eye_sbs + a_ab[u] for u in range(nu)]
    n_sq = int(math.log2(C))
    for i in range(n_sq):
        m = [bd(lp[u]) for u in range(nu)]
        if i == 0:
            lp = [_wdot(lp[u], m[u]) for u in range(nu)]
        elif i < n_sq - 1:
            st = [_wdot(jnp.concatenate([lp[u], tinv[u]], axis=0), m[u]) for u in range(nu)]
            lp = [st[u][:C] for u in range(nu)]
            tinv = [tinv[u] + st[u][C:] for u in range(nu)]
        else:
            tinv = [tinv[u] + _wdot(tinv[u], m[u]) for u in range(nu)]

    w1 = [_wdot(tinv[u], bd(at[u])) for u in range(nu)]
    w2 = [_wdot(tinv[u], bd(av[u][:C])) for u in range(nu)]
    q = [rt[u] + _wdot(a_rb[u], bd(w1[u])) for u in range(nu)]
    o_loc = [_wdot(a_rb[u], bd(w2[u])) + av[u][C:] for u in range(nu)]
    gh = [_wdot_tn(jnp.concatenate([bh[u], kh[u]], axis=0),
                   jnp.concatenate(
                       [jnp.concatenate([w1[u], w2[u]], axis=1),
                        jnp.concatenate([jnp.zeros_like(vv[u]), vv[u]], axis=1)], axis=0))
          for u in range(nu)]
    g_bd = [jnp.where(bd_mask, gh[u][:, :GROUP], 0.0)
            + jnp.where(eye_bd, jnp.concatenate([pcu[u]] * HEADS_PER_GROUP, axis=0), 0.0)
            for u in range(nu)]
    h_bd = [jnp.where(bd_mask, gh[u][:, GROUP:], 0.0) for u in range(nu)]

    state = [s_ref[gi] for gi in range(n_groups)]
    o_rows = []
    for c in range(n_chunks):
        o_cols = []
        for gi in range(n_groups):
            u = c * n_groups + gi
            qs = _wdot(jnp.concatenate([q[u], g_bd[u]], axis=0), state[gi])
            o_cols.append(qs[:C] + o_loc[u])
            state[gi] = qs[C:] + h_bd[u]
        o_rows.append(jnp.concatenate(o_cols, axis=1))
    for gi in range(n_groups):
        s_ref[gi] = state[gi]
    o = jnp.concatenate(o_rows, axis=0)

    inv_n = 1.0 / HEAD
    mean = _mm(o, seg) * inv_n
    d = o - mean
    var = _mm(d * d, seg) * inv_n
    on = d * lax.rsqrt(var + GN_EPS) * gnw_ref[...] + gnb_ref[...]
    bonus = _mm(r * k2 * rk_ref[...], seg) * v
    y_rwkv = (on + bonus) * g

    y = jnp.concatenate([y_conv, y_rwkv], axis=1)
    o_ref[0] = x + _mm(y, wout_ref[...])


def _ffn_kernel(x_ref, p_ref, gffn_ref, wup_ref, cwt_ref, cb_ref, wdn_ref, wproj_ref,
                gple_ref, ggate_ref, wgate_ref, gfin_ref, o_ref, utail, *, ts, dff, final):
    @pl.when(pl.program_id(1) == 0)
    def _():
        utail[...] = jnp.zeros(utail.shape, F32)

    x = x_ref[0]
    h = _rms(x, gffn_ref[...]).astype(BF16)
    e = _rms(_mm(p_ref[0], wproj_ref[...]), gple_ref[...])

    n_ff = dff // FF_CHUNK

    def up(j):
        cols = slice(2 * j * FF_CHUNK, 2 * (j + 1) * FF_CHUNK)
        return jnp.dot(h, wup_ref[:, cols], preferred_element_type=F32)

    def activation(j, u):
        cols = slice(2 * j * FF_CHUNK, 2 * (j + 1) * FF_CHUNK)
        u_prev = utail[:, cols]
        uc = (u * cwt_ref[2:3, cols]
              + _delay_rows(u, u_prev, 1) * cwt_ref[1:2, cols]
              + _delay_rows(u, u_prev, 2) * cwt_ref[0:1, cols]
              + cb_ref[:, cols])
        utail[:, cols] = u[ts - HALO:, :]
        gate, val = uc[:, :FF_CHUNK], uc[:, FF_CHUNK:]
        return (gate * _sigmoid(gate) * val).astype(BF16)

    acc = jnp.zeros_like(x)
    u_next = up(0)
    for j in range(n_ff):
        u = u_next
        if j + 1 < n_ff:
            u_next = up(j + 1)
        act = activation(j, u)
        acc = acc + jnp.dot(act, wdn_ref[j * FF_CHUNK:(j + 1) * FF_CHUNK, :],
                            preferred_element_type=F32)
    x2 = x + acc

    gate = _sigmoid(_mm(_rms(x2, ggate_ref[...]), wgate_ref[...]))
    x3 = x2 + gate * e
    o_ref[0] = _rms(x3, gfin_ref[...]) if final else x3


def _const_spec(shape):
    nd = len(shape)
    return pl.BlockSpec(shape, lambda b, s: (0,) * nd, pipeline_mode=pl.Buffered(1))


def _row(vec):
    return vec.reshape(1, -1).astype(F32)


def _pair_cols(w, dff):
    n = w.shape[0]
    return (w.reshape(n, 2, dff // FF_CHUNK, FF_CHUNK).transpose(0, 2, 1, 3)
            .reshape(n, 2 * dff))


def kernel(x, p, mix_norm_g, w_in, conv_mix_w, rwkv_mu, rwkv_w0, rwkv_w_up, rwkv_a0, rwkv_a_up, rwkv_g_up, rwkv_k_k, rwkv_k_a, rwkv_r_k, rwkv_gn_w, rwkv_gn_b, w_out, ffn_norm_g, ffn_w_up, ffn_conv_w, ffn_conv_b, ffn_w_down, ple_w_proj, ple_norm_g, ple_gate_norm_g, ple_w_gate, final_norm_g):
    depth = w_in.shape[0]
    bsz, seq, d = x.shape
    cw = conv_mix_w.shape[-1]
    rw = rwkv_w0.shape[-1]
    dff = ffn_w_down.shape[1]
    n_dec, n_aicl, n_gate = rwkv_w_up.shape[1], rwkv_a_up.shape[1], rwkv_g_up.shape[1]
    ts = SEQ_TILE
    assert seq % ts == 0 and ts % WKV_CHUNK == 0 and rw % GROUP == 0 and dff % FF_CHUNK == 0
    assert n_dec + n_aicl == 128 and n_gate == 128 and rwkv_r_k.shape[-1] == HEAD
    grid = (bsz, seq // ts)
    act_spec = pl.BlockSpec((1, ts, d), lambda b, s: (b, s, 0))
    params = pltpu.CompilerParams(dimension_semantics=("arbitrary", "arbitrary"),
                                  vmem_limit_bytes=VMEM_LIMIT_BYTES)

    for i in range(depth):
        wup_pad = jnp.concatenate([rwkv_w_up[i], jnp.zeros((n_aicl, rw), F32)], axis=0)
        aup_pad = jnp.concatenate([jnp.zeros((n_dec, rw), F32), rwkv_a_up[i]], axis=0)
        mixer_args = (
            _row(mix_norm_g[i]), w_in[i].astype(BF16), conv_mix_w[i].astype(F32),
            _row(rwkv_mu[i]), _row(rwkv_w0[i]), wup_pad.astype(BF16), _row(rwkv_a0[i]),
            aup_pad.astype(BF16), rwkv_g_up[i].astype(BF16), _row(rwkv_k_k[i]),
            _row(rwkv_k_a[i]), _row(rwkv_r_k[i]), _row(rwkv_gn_w[i]), _row(rwkv_gn_b[i]),
            w_out[i].astype(BF16))
        x = pl.pallas_call(
            functools.partial(_mixer_kernel, ts=ts, cw=cw, rw=rw),
            grid=grid,
            in_specs=[act_spec] + [_const_spec(a.shape) for a in mixer_args],
            out_specs=act_spec,
            out_shape=jax.ShapeDtypeStruct(x.shape, F32),
            scratch_shapes=[
                pltpu.VMEM((HALO, w_in.shape[-1] - 3 * cw), F32),
                pltpu.VMEM((HALO, cw), F32),
                pltpu.VMEM((rw // GROUP, GROUP, GROUP), F32),
            ],
            compiler_params=params,
            name="mixer",
        )(x, *mixer_args)

        ffn_args = (
            _row(ffn_norm_g[i]), _pair_cols(ffn_w_up[i], dff).astype(BF16),
            _pair_cols(ffn_conv_w[i], dff).astype(F32),
            _pair_cols(_row(ffn_conv_b[i]), dff), ffn_w_down[i].astype(BF16),
            ple_w_proj[i].astype(BF16),
            _row(ple_norm_g[i]), _row(ple_gate_norm_g[i]), ple_w_gate[i].astype(BF16))
        fin_g = _row(final_norm_g)
        x = pl.pallas_call(
            functools.partial(_ffn_kernel, ts=ts, dff=dff, final=(i == depth - 1)),
            grid=grid,
            in_specs=[act_spec, pl.BlockSpec((1, ts, p.shape[-1]), lambda b, s: (b, s, 0))]
            + [_const_spec(a.shape) for a in ffn_args] + [_const_spec(fin_g.shape)],
            out_specs=act_spec,
            out_shape=jax.ShapeDtypeStruct(x.shape, F32),
            scratch_shapes=[pltpu.VMEM((HALO, 2 * dff), F32)],
            compiler_params=params,
            name="ffn",
        )(x, p[i], *ffn_args, fin_g)
    return x
```

```python
import functools
import math

import jax
import jax.numpy as jnp
from jax import lax
from jax.experimental import pallas as pl
from jax.experimental.pallas import tpu as pltpu

NORM_EPS = 1e-6
GN_EPS = 64e-5
HEAD = 64
WKV_CHUNK = 64
GROUP = 256
HEADS_PER_GROUP = GROUP // HEAD
HALO = 8
SEQ_TILE = 256
FF_CHUNK = 256
VMEM_LIMIT_BYTES = 56 * 1024 * 1024
WAVE_CHUNKS = 4
CHANNEL_LEAD = 0

F32 = jnp.float32
BF16 = jnp.bfloat16


def _rms(x, g):
    ms = jnp.mean(x * x, axis=-1, keepdims=True)
    return x * lax.rsqrt(ms + NORM_EPS) * g


def _sigmoid(x):
    return 1.0 / (1.0 + jnp.exp(-x))


def _mm(a, b):
    return jnp.dot(a.astype(BF16), b.astype(BF16), preferred_element_type=F32)


def _mm_split2(a_bf16, b):
    b1 = b.astype(BF16)
    b2 = (b - b1.astype(F32)).astype(BF16)
    return (jnp.dot(a_bf16, b1, preferred_element_type=F32)
            + jnp.dot(a_bf16, b2, preferred_element_type=F32))


def _mm_nt(a, b):
    return lax.dot_general(a.astype(BF16), b.astype(BF16), (((1,), (1,)), ((), ())),
                           preferred_element_type=F32)


def _mm_tn(a, b):
    return lax.dot_general(a.astype(BF16), b.astype(BF16), (((0,), (0,)), ((), ())),
                           preferred_element_type=F32)


def _iota2(shape, dim):
    return lax.broadcasted_iota(jnp.int32, shape, dim)


def _delay_rows(cur, prev_tail, n):
    rolled = pltpu.roll(cur, n, axis=0)
    head = jnp.where(_iota2((HALO, cur.shape[1]), 0) < n,
                     pltpu.roll(prev_tail, n, axis=0), rolled[:HALO])
    return jnp.concatenate([head, rolled[HALO:]], axis=0)


def _mixer_stream(x, w, ztail, cxtail, s_ref, result, *, ts, cw, rw):
    n_chunks = ts // WKV_CHUNK
    n_groups = rw // GROUP
    C = WKV_CHUNK

    h = _rms(x, w["g_mix"][...]).astype(BF16)

    zc = jnp.dot(h, w["w_in"][:, 0:3 * cw], preferred_element_type=F32)
    x_in = zc[:, 0:cw]
    b_gate = zc[:, cw:2 * cw]
    c_gate = zc[:, 2 * cw:3 * cw]
    cx = c_gate * x_in
    cx_prev = cxtail[...]
    conv = (cx * w["conv_w"][2:3, :]
            + _delay_rows(cx, cx_prev, 1) * w["conv_w"][1:2, :]
            + _delay_rows(cx, cx_prev, 2) * w["conv_w"][0:1, :])
    y_conv = b_gate * conv
    cxtail[...] = cx[ts - HALO:, :]
    yield

    zr = jnp.dot(h, w["w_in"][:, 3 * cw:], preferred_element_type=F32)
    yield
    zs = _delay_rows(zr, ztail[...], 1)
    ztail[...] = zr[ts - HALO:, :]
    zz = zr + (zs - zr) * w["mu"][...]
    r = zz[:, 0:rw]
    k = zz[:, rw:2 * rw]
    v = zz[:, 2 * rw:3 * rw]
    lora_in = zz[:, 3 * rw:3 * rw + 128]
    gd = zz[:, 3 * rw + 128:3 * rw + 256]

    u_dec = w["w0"][...] + _mm(jnp.tanh(lora_in), w["w_up"][...])
    logw = (-math.exp(-0.5)) * _sigmoid(u_dec)
    a = _sigmoid(w["a0"][...] + _mm(lora_in, w["a_up"][...]))
    g = _mm(_sigmoid(gd), w["g_up"][...])

    seg = (_iota2((rw, rw), 0) // HEAD == _iota2((rw, rw), 1) // HEAD).astype(BF16)

    kk = k * w["k_k"][...]
    kk = kk * lax.rsqrt(jnp.maximum(_mm(kk * kk, seg), 1e-24))
    k2 = k * (1.0 + (a - 1.0) * w["k_a"][...])
    aa = -kk
    bb = kk * a
    yield

    ri = _iota2((ts, ts), 0)
    ci = _iota2((ts, ts), 1)
    tri = jnp.where(((ri // C) == (ci // C)) & (ci <= ri), 1.0, 0.0).astype(BF16)
    cl = _mm_split2(tri, logw)
    ctot = jnp.concatenate(
        [jnp.broadcast_to(cl[(c + 1) * C - 1:(c + 1) * C, :], (C, rw)) for c in range(n_chunks)],
        axis=0)
    ec = jnp.exp(cl)
    eci = jnp.exp(-cl)
    ecm = jnp.exp(cl - logw)
    ecl = jnp.exp(ctot - cl)
    pc = jnp.exp(ctot)
    a_t = aa * ecm
    r_t = r * ec
    b_t = bb * eci
    k_t = k2 * eci
    b_h = bb * ecl
    k_h = k2 * ecl
    yield

    rs = _iota2((C, GROUP), 0)
    cs = _iota2((C, GROUP), 1) % HEAD
    lower_strict = cs < rs
    lower_incl = cs <= rs
    eye_sbs = jnp.where(cs == rs, 1.0, 0.0).astype(F32)
    rb = _iota2((GROUP, GROUP), 0)
    cb = _iota2((GROUP, GROUP), 1)
    bd_mask = (rb // HEAD) == (cb // HEAD)
    eye_bd = rb == cb

    def bd(y):
        yb = y.astype(BF16)
        return jnp.where(bd_mask, jnp.concatenate([yb] * HEADS_PER_GROUP, axis=0),
                         jnp.zeros((), BF16))

    state = [s_ref[gi] for gi in range(n_groups)]
    o_rows = []
    for wave_start in range(0, n_chunks, WAVE_CHUNKS):
        units = [(c, gi) for c in range(wave_start, wave_start + WAVE_CHUNKS)
                 for gi in range(n_groups)]
        nu = len(units)

        def cut(arr):
            return [arr[c * C:(c + 1) * C, gi * GROUP:(gi + 1) * GROUP] for c, gi in units]

        at, rt, bt, kt, bh, kh, vv, pcu = (cut(a_t), cut(r_t), cut(b_t), cut(k_t), cut(b_h),
                                           cut(k_h), cut(v), cut(pc))
        ar = [jnp.concatenate([at[u], rt[u]], axis=0) for u in range(nu)]
        a1 = [_mm_nt(ar[u], bd(bt[u])) for u in range(nu)]
        a2 = [_mm_nt(ar[u], bd(kt[u])) for u in range(nu)]
        a_ab = [jnp.where(lower_strict, a1[u][:C], 0.0) for u in range(nu)]
        a_rb = [jnp.where(lower_incl, a1[u][C:], 0.0) for u in range(nu)]
        a_ak = [jnp.where(lower_strict, a2[u][:C], 0.0) for u in range(nu)]
        a_rk = [jnp.where(lower_incl, a2[u][C:], 0.0) for u in range(nu)]
        yield
        av = [_mm(jnp.concatenate([a_ak[u], a_rk[u]], axis=0), bd(vv[u])) for u in range(nu)]
        yield

        lp = a_ab
        tinv = [eye_sbs + a_ab[u] for u in range(nu)]
        n_sq = int(math.log2(C))
        for i in range(n_sq):
            m = [bd(lp[u]) for u in range(nu)]
            if i == 0:
                lp = [_mm(lp[u], m[u]) for u in range(nu)]
            elif i < n_sq - 1:
                st = [_mm(jnp.concatenate([lp[u], tinv[u]], axis=0), m[u]) for u in range(nu)]
                lp = [st[u][:C] for u in range(nu)]
                tinv = [tinv[u] + st[u][C:] for u in range(nu)]
            else:
                tinv = [tinv[u] + _mm(tinv[u], m[u]) for u in range(nu)]
            yield

        tm = [jnp.concatenate([tinv[u], _mm(a_rb[u], bd(tinv[u]))], axis=0) for u in range(nu)]
        yield
        s1 = [_mm(tm[u], bd(at[u])) for u in range(nu)]
        s2 = [_mm(tm[u], bd(av[u][:C])) for u in range(nu)]
        w1 = [s1[u][:C] for u in range(nu)]
        w2 = [s2[u][:C] for u in range(nu)]
        q = [rt[u] + s1[u][C:] for u in range(nu)]
        o_loc = [s2[u][C:] + av[u][C:] for u in range(nu)]
        yield
        gh = [_mm_tn(jnp.concatenate([bh[u], kh[u]], axis=0),
                     jnp.concatenate(
                         [jnp.concatenate([w1[u], w2[u]], axis=1),
                          jnp.concatenate([jnp.zeros_like(vv[u]), vv[u]], axis=1)], axis=0))
              for u in range(nu)]
        g_bd = [jnp.where(bd_mask, gh[u][:, :GROUP], 0.0)
                + jnp.where(eye_bd, jnp.concatenate([pcu[u]] * HEADS_PER_GROUP, axis=0), 0.0)
                for u in range(nu)]
        h_bd = [jnp.where(bd_mask, gh[u][:, GROUP:], 0.0) for u in range(nu)]
        yield

        for cw_i in range(WAVE_CHUNKS):
            o_cols = []
            for gi in range(n_groups):
                u = cw_i * n_groups + gi
                qs = _mm(jnp.concatenate([q[u], g_bd[u]], axis=0), state[gi])
                o_cols.append(qs[:C] + o_loc[u])
                state[gi] = qs[C:] + h_bd[u]
            o_rows.append(jnp.concatenate(o_cols, axis=1))
            yield
    for gi in range(n_groups):
        s_ref[gi] = state[gi]
    o = jnp.concatenate(o_rows, axis=0)

    inv_n = 1.0 / HEAD
    mean = _mm(o, seg) * inv_n
    d = o - mean
    var = _mm(d * d, seg) * inv_n
    on = d * lax.rsqrt(var + GN_EPS) * w["gn_w"][...] + w["gn_b"][...]
    bonus = _mm(r * k2 * w["r_k"][...], seg) * v
    y_rwkv = (on + bonus) * g
    yield

    y = jnp.concatenate([y_conv, y_rwkv], axis=1)
    result["x1"] = x + _mm(y, w["w_out"][...])


def _channel_stream(x, p, w, utail, result, *, ts, dff, final):
    h = _rms(x, w["g_ffn"][...]).astype(BF16)
    e = _rms(_mm(p, w["w_proj"][...]), w["g_ple"][...])
    n_ff = dff // FF_CHUNK

    def up(j):
        return [jnp.dot(h, w["ffn_up"][:, half * dff + j * FF_CHUNK:
                                       half * dff + (j + 1) * FF_CHUNK],
                        preferred_element_type=F32) for half in range(2)]

    def conv(u, cols):
        u_prev = utail[:, cols]
        uc = (u * w["ffn_cw"][2:3, cols]
              + _delay_rows(u, u_prev, 1) * w["ffn_cw"][1:2, cols]
              + _delay_rows(u, u_prev, 2) * w["ffn_cw"][0:1, cols]
              + w["ffn_cb"][:, cols])
        utail[:, cols] = u[ts - HALO:, :]
        return uc

    def activation(j, u):
        gate, val = [conv(u[half], slice(half * dff + j * FF_CHUNK,
                                         half * dff + (j + 1) * FF_CHUNK)) for half in range(2)]
        return (gate * _sigmoid(gate) * val).astype(BF16)

    acc = jnp.zeros_like(x)
    u_next = up(0)
    yield
    for j in range(n_ff):
        u = u_next
        if j + 1 < n_ff:
            u_next = up(j + 1)
        act = activation(j, u)
        acc = acc + jnp.dot(act, w["ffn_down"][j * FF_CHUNK:(j + 1) * FF_CHUNK, :],
                            preferred_element_type=F32)
        yield
    x2 = x + acc

    gate = _sigmoid(_mm(_rms(x2, w["g_gate"][...]), w["w_gate"][...]))
    x3 = x2 + gate * e
    result["out"] = _rms(x3, w["g_fin"][...]) if final else x3


def _run_interleaved(gen_a, n_a, gen_b, n_b, lead):
    done_b = 0
    for ka in range(n_a):
        next(gen_a, None)
        want_b = min(n_b, -(-((ka + 1) * n_b) // (n_a - lead)))
        while done_b < want_b:
            next(gen_b, None)
            done_b += 1
    for gen in (gen_a, gen_b):
        for _ in gen:
            pass


def _layer_kernel(x_ref, p_ref, *refs, names, ts, cw, rw, dff, n_s, n_tiles, final):
    n_w = len(names)
    w = dict(zip(names, refs[:n_w]))
    o_ref = refs[n_w]
    ztail, cxtail, s_ref, utail, x1buf = refs[n_w + 1:]

    i = pl.program_id(0)
    mix_s = lax.rem(jnp.minimum(i, n_tiles - 1), jnp.int32(n_s))
    ch_s = lax.rem(jnp.maximum(i - 1, 0), jnp.int32(n_s))

    @pl.when(i == 0)
    def _():
        x1buf[...] = jnp.zeros(x1buf.shape, F32)

    @pl.when(mix_s == 0)
    def _():
        ztail[...] = jnp.zeros(ztail.shape, F32)
        cxtail[...] = jnp.zeros(cxtail.shape, F32)
        s_ref[...] = jnp.zeros(s_ref.shape, F32)

    @pl.when(ch_s == 0)
    def _():
        utail[...] = jnp.zeros(utail.shape, F32)

    slot = lax.rem(i, jnp.int32(2))
    result = {}
    mixer = _mixer_stream(x_ref[0], w, ztail, cxtail, s_ref, result, ts=ts, cw=cw, rw=rw)
    channel = _channel_stream(x1buf[1 - slot], p_ref[0], w, utail, result,
                              ts=ts, dff=dff, final=final)
    n_waves = ts // (WKV_CHUNK * WAVE_CHUNKS)
    n_mixer = 6 + n_waves * (5 + int(math.log2(WKV_CHUNK)) + WAVE_CHUNKS)
    n_channel = 2 + dff // FF_CHUNK
    _run_interleaved(mixer, n_mixer, channel, n_channel, CHANNEL_LEAD)
    x1buf[slot] = result["x1"]
    o_ref[0] = result["out"]


def _const_spec(shape):
    nd = len(shape)
    return pl.BlockSpec(shape, lambda i: (0,) * nd, pipeline_mode=pl.Buffered(1))


def _row(vec):
    return vec.reshape(1, -1).astype(F32)


def kernel(x, p, mix_norm_g, w_in, conv_mix_w, rwkv_mu, rwkv_w0, rwkv_w_up, rwkv_a0, rwkv_a_up, rwkv_g_up, rwkv_k_k, rwkv_k_a, rwkv_r_k, rwkv_gn_w, rwkv_gn_b, w_out, ffn_norm_g, ffn_w_up, ffn_conv_w, ffn_conv_b, ffn_w_down, ple_w_proj, ple_norm_g, ple_gate_norm_g, ple_w_gate, final_norm_g):
    depth = w_in.shape[0]
    bsz, seq, d = x.shape
    cw = conv_mix_w.shape[-1]
    rw = rwkv_w0.shape[-1]
    dff = ffn_w_down.shape[1]
    n_dec, n_aicl, n_gate = rwkv_w_up.shape[1], rwkv_a_up.shape[1], rwkv_g_up.shape[1]
    ts = SEQ_TILE
    assert seq % ts == 0 and ts % WKV_CHUNK == 0 and rw % GROUP == 0 and dff % FF_CHUNK == 0
    assert n_dec + n_aicl == 128 and n_gate == 128 and rwkv_r_k.shape[-1] == HEAD
    n_s = seq // ts
    n_tiles = bsz * n_s

    def mixer_tile(i):
        t = jnp.minimum(i, n_tiles - 1)
        return (t // n_s, t % n_s, 0)

    def channel_tile(i):
        t = jnp.maximum(i - 1, 0)
        return (t // n_s, t % n_s, 0)

    for li in range(depth):
        wup_pad = jnp.concatenate([rwkv_w_up[li], jnp.zeros((n_aicl, rw), F32)], axis=0)
        aup_pad = jnp.concatenate([jnp.zeros((n_dec, rw), F32), rwkv_a_up[li]], axis=0)
        weights = {
            "g_mix": _row(mix_norm_g[li]), "w_in": w_in[li].astype(BF16),
            "conv_w": conv_mix_w[li].astype(F32), "mu": _row(rwkv_mu[li]),
            "w0": _row(rwkv_w0[li]), "w_up": wup_pad.astype(BF16), "a0": _row(rwkv_a0[li]),
            "a_up": aup_pad.astype(BF16), "g_up": rwkv_g_up[li].astype(BF16),
            "k_k": _row(rwkv_k_k[li]), "k_a": _row(rwkv_k_a[li]), "r_k": _row(rwkv_r_k[li]),
            "gn_w": _row(rwkv_gn_w[li]), "gn_b": _row(rwkv_gn_b[li]),
            "w_out": w_out[li].astype(BF16),
            "g_ffn": _row(ffn_norm_g[li]), "ffn_up": ffn_w_up[li].astype(BF16),
            "ffn_cw": ffn_conv_w[li].astype(F32), "ffn_cb": _row(ffn_conv_b[li]),
            "ffn_down": ffn_w_down[li].astype(BF16), "w_proj": ple_w_proj[li].astype(BF16),
            "g_ple": _row(ple_norm_g[li]), "g_gate": _row(ple_gate_norm_g[li]),
            "w_gate": ple_w_gate[li].astype(BF16), "g_fin": _row(final_norm_g),
        }
        names = tuple(weights)
        x = pl.pallas_call(
            functools.partial(_layer_kernel, names=names, ts=ts, cw=cw, rw=rw, dff=dff,
                              n_s=n_s, n_tiles=n_tiles, final=(li == depth - 1)),
            grid=(n_tiles + 1,),
            in_specs=[pl.BlockSpec((1, ts, d), mixer_tile),
                      pl.BlockSpec((1, ts, p.shape[-1]), channel_tile)]
            + [_const_spec(weights[n].shape) for n in names],
            out_specs=pl.BlockSpec((1, ts, d), channel_tile),
            out_shape=jax.ShapeDtypeStruct(x.shape, F32),
            scratch_shapes=[
                pltpu.VMEM((HALO, w_in.shape[-1] - 3 * cw), F32),
                pltpu.VMEM((HALO, cw), F32),
                pltpu.VMEM((rw // GROUP, GROUP, GROUP), F32),
                pltpu.VMEM((HALO, 2 * dff), F32),
                pltpu.VMEM((2, ts, d), F32),
            ],
            compiler_params=pltpu.CompilerParams(dimension_semantics=("arbitrary",),
                                                 vmem_limit_bytes=VMEM_LIMIT_BYTES),
            name="layer",
        )(x, p[li], *[weights[n] for n in names])
    return x
```

```python
import functools
import math

import jax
import jax.numpy as jnp
from jax import lax
from jax.experimental import pallas as pl
from jax.experimental.pallas import tpu as pltpu

NORM_EPS = 1e-6
GN_EPS = 64e-5
HEAD = 64
WKV_CHUNK = 64
GROUP = 256
HEADS_PER_GROUP = GROUP // HEAD
HALO = 8
SEQ_TILE = 256
FF_CHUNK = 256
VMEM_LIMIT_BYTES = 56 * 1024 * 1024
WAVE_CHUNKS = 4
CHANNEL_LEAD = 0

F32 = jnp.float32
BF16 = jnp.bfloat16


def _rms(x, g):
    ms = jnp.mean(x * x, axis=-1, keepdims=True)
    return x * lax.rsqrt(ms + NORM_EPS) * g


def _sigmoid(x):
    return 1.0 / (1.0 + jnp.exp(-x))


def _mm(a, b):
    return jnp.dot(a.astype(BF16), b.astype(BF16), preferred_element_type=F32)


def _mm_split2(a_bf16, b):
    b1 = b.astype(BF16)
    b2 = (b - b1.astype(F32)).astype(BF16)
    return (jnp.dot(a_bf16, b1, preferred_element_type=F32)
            + jnp.dot(a_bf16, b2, preferred_element_type=F32))


def _mm_nt(a, b):
    return lax.dot_general(a.astype(BF16), b.astype(BF16), (((1,), (1,)), ((), ())),
                           preferred_element_type=F32)


def _iota2(shape, dim):
    return lax.broadcasted_iota(jnp.int32, shape, dim)


def _delay_rows(cur, prev_tail, n):
    rolled = pltpu.roll(cur, n, axis=0)
    head = jnp.where(_iota2((HALO, cur.shape[1]), 0) < n,
                     pltpu.roll(prev_tail, n, axis=0), rolled[:HALO])
    return jnp.concatenate([head, rolled[HALO:]], axis=0)


def _mixer_stream(x, w, ztail, cxtail, s_ref, result, *, ts, cw, rw):
    n_chunks = ts // WKV_CHUNK
    n_groups = rw // GROUP
    C = WKV_CHUNK

    h = _rms(x, w["g_mix"][...]).astype(BF16)

    zc = jnp.dot(h, w["w_in"][:, 0:3 * cw], preferred_element_type=F32)
    x_in = zc[:, 0:cw]
    b_gate = zc[:, cw:2 * cw]
    c_gate = zc[:, 2 * cw:3 * cw]
    cx = c_gate * x_in
    cx_prev = cxtail[...]
    conv = (cx * w["conv_w"][2:3, :]
            + _delay_rows(cx, cx_prev, 1) * w["conv_w"][1:2, :]
            + _delay_rows(cx, cx_prev, 2) * w["conv_w"][0:1, :])
    y_conv = b_gate * conv
    cxtail[...] = cx[ts - HALO:, :]
    yield

    zr = jnp.dot(h, w["w_in"][:, 3 * cw:], preferred_element_type=F32)
    yield
    zs = _delay_rows(zr, ztail[...], 1)
    ztail[...] = zr[ts - HALO:, :]
    zz = zr + (zs - zr) * w["mu"][...]
    yield
    r = zz[:, 0:rw]
    k = zz[:, rw:2 * rw]
    v = zz[:, 2 * rw:3 * rw]
    lora_in = zz[:, 3 * rw:3 * rw + 128]
    gd = zz[:, 3 * rw + 128:3 * rw + 256]

    u_dec = w["w0"][...] + _mm(jnp.tanh(lora_in), w["w_up"][...])
    logw = (-math.exp(-0.5)) * _sigmoid(u_dec)
    a = _sigmoid(w["a0"][...] + _mm(lora_in, w["a_up"][...]))
    g = _mm(_sigmoid(gd), w["g_up"][...])
    yield

    seg = (_iota2((rw, rw), 0) // HEAD == _iota2((rw, rw), 1) // HEAD).astype(BF16)

    kk = k * w["k_k"][...]
    kk = kk * lax.rsqrt(jnp.maximum(_mm(kk * kk, seg), 1e-24))
    k2 = k * (1.0 + (a - 1.0) * w["k_a"][...])
    aa = -kk
    bb = kk * a
    yield

    ri = _iota2((ts, ts), 0)
    ci = _iota2((ts, ts), 1)
    tri = jnp.where(((ri // C) == (ci // C)) & (ci <= ri), 1.0, 0.0).astype(BF16)
    cl = _mm_split2(tri, logw)
    ctot = jnp.concatenate(
        [jnp.broadcast_to(cl[(c + 1) * C - 1:(c + 1) * C, :], (C, rw)) for c in range(n_chunks)],
        axis=0)
    yield
    ec = jnp.exp(cl)
    eci = jnp.exp(-cl)
    ecm = jnp.exp(cl - logw)
    ecl = jnp.exp(ctot - cl)
    pc = jnp.exp(ctot)
    yield
    a_t = aa * ecm
    r_t = r * ec
    b_t = bb * eci
    k_t = k2 * eci
    b_h = bb * ecl
    k_h = k2 * ecl
    yield

    rs = _iota2((C, GROUP), 0)
    cs = _iota2((C, GROUP), 1) % HEAD
    lower_strict = cs < rs
    lower_incl = cs <= rs
    eye_sbs = jnp.where(cs == rs, 1.0, 0.0).astype(F32)
    rb = _iota2((GROUP, GROUP), 0)
    cb = _iota2((GROUP, GROUP), 1)
    bd_mask = (rb // HEAD) == (cb // HEAD)
    pair_head = (_iota2((2 * C, GROUP), 1) % (2 * HEAD)) // HEAD

    def bd(y):
        yb = y.astype(BF16)
        return jnp.where(bd_mask, jnp.concatenate([yb] * HEADS_PER_GROUP, axis=0),
                         jnp.zeros((), BF16))

    state = [s_ref[gi] for gi in range(n_groups)]
    o_rows = []
    for wave_start in range(0, n_chunks, WAVE_CHUNKS):
        units = [(c, gi) for c in range(wave_start, wave_start + WAVE_CHUNKS)
                 for gi in range(n_groups)]
        nu = len(units)

        def cut(arr):
            return [arr[c * C:(c + 1) * C, gi * GROUP:(gi + 1) * GROUP] for c, gi in units]

        at, rt, bt, kt, bh, kh, vv, pcu = (cut(a_t), cut(r_t), cut(b_t), cut(k_t), cut(b_h),
                                           cut(k_h), cut(v), cut(pc))
        ar = [jnp.concatenate([at[u], rt[u]], axis=0) for u in range(nu)]
        a1 = [_mm_nt(ar[u], bd(bt[u])) for u in range(nu)]
        a2 = [_mm_nt(ar[u], bd(kt[u])) for u in range(nu)]
        a_ab = [jnp.where(lower_strict, a1[u][:C], 0.0) for u in range(nu)]
        a_rb = [jnp.where(lower_incl, a1[u][C:], 0.0) for u in range(nu)]
        a_ak = [jnp.where(lower_strict, a2[u][:C], 0.0) for u in range(nu)]
        a_rk = [jnp.where(lower_incl, a2[u][C:], 0.0) for u in range(nu)]
        yield
        av = [_mm(jnp.concatenate([a_ak[u], a_rk[u]], axis=0), bd(vv[u])) for u in range(nu)]
        yield

        lp = a_ab
        tinv = [eye_sbs + a_ab[u] for u in range(nu)]
        n_sq = int(math.log2(C))
        for i in range(n_sq):
            m = [bd(lp[u]) for u in range(nu)]
            if i == 0:
                lp = [_mm(lp[u], m[u]) for u in range(nu)]
            elif i < n_sq - 1:
                st = [_mm(jnp.concatenate([lp[u], tinv[u]], axis=0), m[u]) for u in range(nu)]
                lp = [st[u][:C] for u in range(nu)]
                tinv = [tinv[u] + st[u][C:] for u in range(nu)]
            else:
                tinv = [tinv[u] + _mm(tinv[u], m[u]) for u in range(nu)]
            yield

        tm = [jnp.concatenate([tinv[u], _mm(a_rb[u], bd(tinv[u]))], axis=0) for u in range(nu)]
        yield
        s1 = [_mm(tm[u], bd(at[u])) for u in range(nu)]
        s2 = [_mm(tm[u], bd(av[u][:C])) for u in range(nu)]
        w1 = [s1[u][:C] for u in range(nu)]
        w2 = [s2[u][:C] for u in range(nu)]
        q = [rt[u] + s1[u][C:] for u in range(nu)]
        o_loc = [s2[u][C:] + av[u][C:] for u in range(nu)]
        yield
        g_sbs, h_sbs = [], []
        for u in range(nu):
            xt = jnp.concatenate([bh[u], kh[u]], axis=0).T
            outs = []
            for pair in range(HEADS_PER_GROUP // 2):
                lhs = jnp.concatenate([xt[(2 * pair) * HEAD:(2 * pair + 1) * HEAD],
                                       xt[(2 * pair + 1) * HEAD:(2 * pair + 2) * HEAD]], axis=1)
                lanes = slice(pair * 2 * HEAD, (pair + 1) * 2 * HEAD)
                blk = jnp.concatenate(
                    [jnp.concatenate([w1[u][:, lanes], w2[u][:, lanes]], axis=1),
                     jnp.concatenate([jnp.zeros((C, 2 * HEAD), F32), vv[u][:, lanes]], axis=1)],
                    axis=0).astype(BF16)
                zero = jnp.zeros((), BF16)
                rhs = jnp.concatenate([jnp.where(pair_head == 0, blk, zero),
                                       jnp.where(pair_head == 1, blk, zero)], axis=0)
                outs.append(_mm(lhs, rhs))
            g_sbs.append(jnp.concatenate([o_[:, :2 * HEAD] for o_ in outs], axis=1)
                         + eye_sbs * pcu[u])
            h_sbs.append(jnp.concatenate([o_[:, 2 * HEAD:] for o_ in outs], axis=1))
        yield

        for cw_i in range(WAVE_CHUNKS):
            o_cols = []
            for gi in range(n_groups):
                u = cw_i * n_groups + gi
                qs = _mm(jnp.concatenate([q[u], g_sbs[u]], axis=0), bd(state[gi]))
                o_cols.append(qs[:C] + o_loc[u])
                state[gi] = qs[C:] + h_sbs[u]
            o_rows.append(jnp.concatenate(o_cols, axis=1))
            yield
    for gi in range(n_groups):
        s_ref[gi] = state[gi]
    o = jnp.concatenate(o_rows, axis=0)

    inv_n = 1.0 / HEAD
    mean = _mm(o, seg) * inv_n
    d = o - mean
    var = _mm(d * d, seg) * inv_n
    on = d * lax.rsqrt(var + GN_EPS) * w["gn_w"][...] + w["gn_b"][...]
    bonus = _mm(r * k2 * w["r_k"][...], seg) * v
    y_rwkv = (on + bonus) * g
    yield

    y = jnp.concatenate([y_conv, y_rwkv], axis=1)
    result["x1"] = x + _mm(y, w["w_out"][...])


def _channel_stream(x, p, w, utail, result, *, ts, dff, final):
    h = _rms(x, w["g_ffn"][...]).astype(BF16)
    e = _rms(_mm(p, w["w_proj"][...]), w["g_ple"][...])
    n_ff = dff // FF_CHUNK

    def up(j):
        return [jnp.dot(h, w["ffn_up"][:, half * dff + j * FF_CHUNK:
                                       half * dff + (j + 1) * FF_CHUNK],
                        preferred_element_type=F32) for half in range(2)]

    def conv(u, cols):
        u_prev = utail[:, cols]
        uc = (u * w["ffn_cw"][2:3, cols]
              + _delay_rows(u, u_prev, 1) * w["ffn_cw"][1:2, cols]
              + _delay_rows(u, u_prev, 2) * w["ffn_cw"][0:1, cols]
              + w["ffn_cb"][:, cols])
        utail[:, cols] = u[ts - HALO:, :]
        return uc

    def activation(j, u):
        gate, val = [conv(u[half], slice(half * dff + j * FF_CHUNK,
                                         half * dff + (j + 1) * FF_CHUNK)) for half in range(2)]
        return (gate * _sigmoid(gate) * val).astype(BF16)

    acc = x
    u_next = up(0)
    yield
    for j in range(n_ff):
        u = u_next
        if j + 1 < n_ff:
            u_next = up(j + 1)
            yield
        act = activation(j, u)
        yield
        acc = acc + jnp.dot(act, w["ffn_down"][j * FF_CHUNK:(j + 1) * FF_CHUNK, :],
                            preferred_element_type=F32)
        yield
    x2 = acc

    gate = _sigmoid(_mm(_rms(x2, w["g_gate"][...]), w["w_gate"][...]))
    x3 = x2 + gate * e
    result["out"] = _rms(x3, w["g_fin"][...]) if final else x3


def _run_interleaved(gen_a, n_a, gen_b, n_b, lead):
    done_b = 0
    for ka in range(n_a):
        next(gen_a, None)
        want_b = min(n_b, -(-((ka + 1) * n_b) // (n_a - lead)))
        while done_b < want_b:
            next(gen_b, None)
            done_b += 1
    for gen in (gen_a, gen_b):
        for _ in gen:
            pass


def _layer_kernel(x_ref, p_ref, *refs, names, ts, cw, rw, dff, n_s, n_tiles, final):
    n_w = len(names)
    w = dict(zip(names, refs[:n_w]))
    o_ref = refs[n_w]
    ztail, cxtail, s_ref, utail, x1buf = refs[n_w + 1:]

    i = pl.program_id(0)
    mix_s = lax.rem(jnp.minimum(i, n_tiles - 1), jnp.int32(n_s))
    ch_s = lax.rem(jnp.maximum(i - 1, 0), jnp.int32(n_s))

    @pl.when(i == 0)
    def _():
        x1buf[...] = jnp.zeros(x1buf.shape, F32)

    @pl.when(mix_s == 0)
    def _():
        ztail[...] = jnp.zeros(ztail.shape, F32)
        cxtail[...] = jnp.zeros(cxtail.shape, F32)
        s_ref[...] = jnp.zeros(s_ref.shape, F32)

    @pl.when(ch_s == 0)
    def _():
        utail[...] = jnp.zeros(utail.shape, F32)

    slot = lax.rem(i, jnp.int32(2))
    result = {}
    mixer = _mixer_stream(x_ref[0], w, ztail, cxtail, s_ref, result, ts=ts, cw=cw, rw=rw)
    channel = _channel_stream(x1buf[1 - slot], p_ref[0], w, utail, result,
                              ts=ts, dff=dff, final=final)
    n_waves = ts // (WKV_CHUNK * WAVE_CHUNKS)
    n_mixer = 10 + n_waves * (5 + int(math.log2(WKV_CHUNK)) + WAVE_CHUNKS)
    n_channel = 1 + 3 * (dff // FF_CHUNK)
    _run_interleaved(mixer, n_mixer, channel, n_channel, CHANNEL_LEAD)
    x1buf[slot] = result["x1"]
    o_ref[0] = result["out"]


def _const_spec(shape):
    nd = len(shape)
    return pl.BlockSpec(shape, lambda i: (0,) * nd, pipeline_mode=pl.Buffered(1))


def _row(vec):
    return vec.reshape(1, -1).astype(F32)


def kernel(x, p, mix_norm_g, w_in, conv_mix_w, rwkv_mu, rwkv_w0, rwkv_w_up, rwkv_a0, rwkv_a_up, rwkv_g_up, rwkv_k_k, rwkv_k_a, rwkv_r_k, rwkv_gn_w, rwkv_gn_b, w_out, ffn_norm_g, ffn_w_up, ffn_conv_w, ffn_conv_b, ffn_w_down, ple_w_proj, ple_norm_g, ple_gate_norm_g, ple_w_gate, final_norm_g):
    depth = w_in.shape[0]
    bsz, seq, d = x.shape
    cw = conv_mix_w.shape[-1]
    rw = rwkv_w0.shape[-1]
    dff = ffn_w_down.shape[1]
    n_dec, n_aicl, n_gate = rwkv_w_up.shape[1], rwkv_a_up.shape[1], rwkv_g_up.shape[1]
    ts = SEQ_TILE
    assert seq % ts == 0 and ts % WKV_CHUNK == 0 and rw % GROUP == 0 and dff % FF_CHUNK == 0
    assert n_dec + n_aicl == 128 and n_gate == 128 and rwkv_r_k.shape[-1] == HEAD
    n_s = seq // ts
    n_tiles = bsz * n_s

    def mixer_tile(i):
        t = jnp.minimum(i, n_tiles - 1)
        return (t // n_s, t % n_s, 0)

    def channel_tile(i):
        t = jnp.maximum(i - 1, 0)
        return (t // n_s, t % n_s, 0)

    for li in range(depth):
        wup_pad = jnp.concatenate([rwkv_w_up[li], jnp.zeros((n_aicl, rw), F32)], axis=0)
        aup_pad = jnp.concatenate([jnp.zeros((n_dec, rw), F32), rwkv_a_up[li]], axis=0)
        weights = {
            "g_mix": _row(mix_norm_g[li]), "w_in": w_in[li].astype(BF16),
            "conv_w": conv_mix_w[li].astype(F32), "mu": _row(rwkv_mu[li]),
            "w0": _row(rwkv_w0[li]), "w_up": wup_pad.astype(BF16), "a0": _row(rwkv_a0[li]),
            "a_up": aup_pad.astype(BF16), "g_up": rwkv_g_up[li].astype(BF16),
            "k_k": _row(rwkv_k_k[li]), "k_a": _row(rwkv_k_a[li]), "r_k": _row(rwkv_r_k[li]),
            "gn_w": _row(rwkv_gn_w[li]), "gn_b": _row(rwkv_gn_b[li]),
            "w_out": w_out[li].astype(BF16),
            "g_ffn": _row(ffn_norm_g[li]), "ffn_up": ffn_w_up[li].astype(BF16),
            "ffn_cw": ffn_conv_w[li].astype(F32), "ffn_cb": _row(ffn_conv_b[li]),
            "ffn_down": ffn_w_down[li].astype(BF16), "w_proj": ple_w_proj[li].astype(BF16),
            "g_ple": _row(ple_norm_g[li]), "g_gate": _row(ple_gate_norm_g[li]),
            "w_gate": ple_w_gate[li].astype(BF16), "g_fin": _row(final_norm_g),
        }
        names = tuple(weights)
        x = pl.pallas_call(
            functools.partial(_layer_kernel, names=names, ts=ts, cw=cw, rw=rw, dff=dff,
                              n_s=n_s, n_tiles=n_tiles, final=(li == depth - 1)),
            grid=(n_tiles + 1,),
            in_specs=[pl.BlockSpec((1, ts, d), mixer_tile),
                      pl.BlockSpec((1, ts, p.shape[-1]), channel_tile)]
            + [_const_spec(weights[n].shape) for n in names],
            out_specs=pl.BlockSpec((1, ts, d), channel_tile),
            out_shape=jax.ShapeDtypeStruct(x.shape, F32),
            scratch_shapes=[
                pltpu.VMEM((HALO, w_in.shape[-1] - 3 * cw), F32),
                pltpu.VMEM((HALO, cw), F32),
                pltpu.VMEM((rw // GROUP, HEAD, GROUP), F32),
                pltpu.VMEM((HALO, 2 * dff), F32),
                pltpu.VMEM((2, ts, d), F32),
            ],
            compiler_params=pltpu.CompilerParams(dimension_semantics=("arbitrary",),
                                                 vmem_limit_bytes=VMEM_LIMIT_BYTES),
            name="layer",
        )(x, p[li], *[weights[n] for n in names])
    return x
```

```python
import functools
import math

import jax
import jax.numpy as jnp
from jax import lax
from jax.experimental import pallas as pl
from jax.experimental.pallas import tpu as pltpu

NORM_EPS = 1e-6
GN_EPS = 64e-5
HEAD = 64
WKV_CHUNK = 64
GROUP = 256
HEADS_PER_GROUP = GROUP // HEAD
HALO = 8
LANES = 128
SEQ_TILE = 256
FF_CHUNK = 256
VMEM_LIMIT_BYTES = 56 * 1024 * 1024
WAVE_CHUNKS = 4
CHANNEL_AHEAD = 4

F32 = jnp.float32
BF16 = jnp.bfloat16


def _rms(x, g):
    ms = jnp.mean(x * x, axis=-1, keepdims=True)
    return x * lax.rsqrt(ms + NORM_EPS) * g


def _sigmoid(x):
    return 1.0 / (1.0 + jnp.exp(-x))


def _mm(a, b):
    return jnp.dot(a.astype(BF16), b.astype(BF16), preferred_element_type=F32)


def _mm_split2(a_bf16, b):
    b1 = b.astype(BF16)
    b2 = (b - b1.astype(F32)).astype(BF16)
    return (jnp.dot(a_bf16, b1, preferred_element_type=F32)
            + jnp.dot(a_bf16, b2, preferred_element_type=F32))


def _mm_nt(a, b):
    return lax.dot_general(a.astype(BF16), b.astype(BF16), (((1,), (1,)), ((), ())),
                           preferred_element_type=F32)


def _iota2(shape, dim):
    return lax.broadcasted_iota(jnp.int32, shape, dim)


def _delayed(hist, tile0, cur, delays):
    ts = cur.shape[0]
    n = cur.shape[1] // LANES
    for c in range(n):
        hist[tile0 + c, HALO:HALO + ts, :] = cur[:, c * LANES:(c + 1) * LANES]
    outs = [jnp.concatenate([hist[tile0 + c, HALO - d:HALO - d + ts, :] for c in range(n)], axis=1)
            for d in delays]
    for c in range(n):
        hist[tile0 + c, 0:HALO, :] = hist[tile0 + c, ts:ts + HALO, :]
    return outs


def _mixer_stream(x, w, zhist, cxhist, s_ref, result, *, ts, cw, rw):
    n_chunks = ts // WKV_CHUNK
    n_groups = rw // GROUP
    C = WKV_CHUNK

    h = _rms(x, w["g_mix"][...]).astype(BF16)

    zc = jnp.dot(h, w["w_in"][:, 0:3 * cw], preferred_element_type=F32)
    x_in = zc[:, 0:cw]
    b_gate = zc[:, cw:2 * cw]
    c_gate = zc[:, 2 * cw:3 * cw]
    cx = c_gate * x_in
    cx1, cx2 = _delayed(cxhist, 0, cx, (1, 2))
    conv = cx * w["conv_w"][2:3, :] + cx1 * w["conv_w"][1:2, :] + cx2 * w["conv_w"][0:1, :]
    y_conv = b_gate * conv
    yield

    zr = jnp.dot(h, w["w_in"][:, 3 * cw:], preferred_element_type=F32)
    yield
    (zs,) = _delayed(zhist, 0, zr, (1,))
    zz = zr + (zs - zr) * w["mu"][...]
    yield
    r = zz[:, 0:rw]
    k = zz[:, rw:2 * rw]
    v = zz[:, 2 * rw:3 * rw]
    lora_in = zz[:, 3 * rw:3 * rw + 128]
    gd = zz[:, 3 * rw + 128:3 * rw + 256]

    u_dec = w["w0"][...] + _mm(jnp.tanh(lora_in), w["w_up"][...])
    logw = (-math.exp(-0.5)) * _sigmoid(u_dec)
    a = _sigmoid(w["a0"][...] + _mm(lora_in, w["a_up"][...]))
    g = _mm(_sigmoid(gd), w["g_up"][...])
    yield

    seg = (_iota2((rw, rw), 0) // HEAD == _iota2((rw, rw), 1) // HEAD).astype(BF16)

    kk = k * w["k_k"][...]
    kk = kk * lax.rsqrt(jnp.maximum(_mm(kk * kk, seg), 1e-24))
    k2 = k * (1.0 + (a - 1.0) * w["k_a"][...])
    aa = -kk
    bb = kk * a
    yield

    ri = _iota2((ts, ts), 0)
    ci = _iota2((ts, ts), 1)
    tri = jnp.where(((ri // C) == (ci // C)) & (ci <= ri), 1.0, 0.0).astype(BF16)
    cl = _mm_split2(tri, logw)
    yield
    ec = jnp.exp(cl)
    eci = 1.0 / ec
    ecm = jnp.exp(cl - logw)
    pc = jnp.concatenate(
        [jnp.broadcast_to(ec[(c + 1) * C - 1:(c + 1) * C, :], (C, rw)) for c in range(n_chunks)],
        axis=0)
    yield
    a_t = aa * ecm
    r_t = r * ec
    b_t = bb * eci
    k_t = k2 * eci
    b_h = b_t * pc
    k_h = k_t * pc
    yield

    rs = _iota2((C, GROUP), 0)
    cs = _iota2((C, GROUP), 1) % HEAD
    lower_strict = cs < rs
    lower_incl = cs <= rs
    eye_sbs = jnp.where(cs == rs, 1.0, 0.0).astype(F32)
    rb = _iota2((GROUP, GROUP), 0)
    cb = _iota2((GROUP, GROUP), 1)
    bd_mask = (rb // HEAD) == (cb // HEAD)
    pair_head = (_iota2((2 * C, GROUP), 1) % (2 * HEAD)) // HEAD

    def bd(y):
        yb = y.astype(BF16)
        return jnp.where(bd_mask, jnp.concatenate([yb] * HEADS_PER_GROUP, axis=0),
                         jnp.zeros((), BF16))

    state = [s_ref[gi] for gi in range(n_groups)]
    o_rows = []
    for wave_start in range(0, n_chunks, WAVE_CHUNKS):
        units = [(c, gi) for c in range(wave_start, wave_start + WAVE_CHUNKS)
                 for gi in range(n_groups)]
        nu = len(units)

        def cut(arr):
            return [arr[c * C:(c + 1) * C, gi * GROUP:(gi + 1) * GROUP] for c, gi in units]

        at, rt, bt, kt, bh, kh, vv, pcu = (cut(a_t), cut(r_t), cut(b_t), cut(k_t), cut(b_h),
                                           cut(k_h), cut(v), cut(pc))
        ar = [jnp.concatenate([at[u], rt[u]], axis=0) for u in range(nu)]
        a1 = [_mm_nt(ar[u], bd(bt[u])) for u in range(nu)]
        a2 = [_mm_nt(ar[u], bd(kt[u])) for u in range(nu)]
        a_ab = [jnp.where(lower_strict, a1[u][:C], 0.0) for u in range(nu)]
        a_rb = [jnp.where(lower_incl, a1[u][C:], 0.0) for u in range(nu)]
        a_ak = [jnp.where(lower_strict, a2[u][:C], 0.0) for u in range(nu)]
        a_rk = [jnp.where(lower_incl, a2[u][C:], 0.0) for u in range(nu)]
        yield
        av = [_mm(jnp.concatenate([a_ak[u], a_rk[u]], axis=0), bd(vv[u])) for u in range(nu)]
        yield

        lp = a_ab
        tinv = [eye_sbs + a_ab[u] for u in range(nu)]
        n_sq = int(math.log2(C))
        for i in range(n_sq):
            m = [bd(lp[u]) for u in range(nu)]
            if i == 0:
                lp = [_mm(lp[u], m[u]) for u in range(nu)]
            elif i < n_sq - 1:
                st = [_mm(jnp.concatenate([lp[u], tinv[u]], axis=0), m[u]) for u in range(nu)]
                lp = [st[u][:C] for u in range(nu)]
                tinv = [tinv[u] + st[u][C:] for u in range(nu)]
            else:
                tinv = [tinv[u] + _mm(tinv[u], m[u]) for u in range(nu)]
            yield

        tm = [jnp.concatenate([tinv[u], _mm(a_rb[u], bd(tinv[u]))], axis=0) for u in range(nu)]
        yield
        s1 = [_mm(tm[u], bd(at[u])) for u in range(nu)]
        s2 = [_mm(tm[u], bd(av[u][:C])) for u in range(nu)]
        w1 = [s1[u][:C] for u in range(nu)]
        w2 = [s2[u][:C] for u in range(nu)]
        q = [rt[u] + s1[u][C:] for u in range(nu)]
        o_loc = [s2[u][C:] + av[u][C:] for u in range(nu)]
        yield
        g_sbs, h_sbs = [], []
        for u in range(nu):
            xt = jnp.concatenate([bh[u], kh[u]], axis=0).T
            outs = []
            for pair in range(HEADS_PER_GROUP // 2):
                lhs = jnp.concatenate([xt[(2 * pair) * HEAD:(2 * pair + 1) * HEAD],
                                       xt[(2 * pair + 1) * HEAD:(2 * pair + 2) * HEAD]], axis=1)
                lanes = slice(pair * 2 * HEAD, (pair + 1) * 2 * HEAD)
                blk = jnp.concatenate(
                    [jnp.concatenate([w1[u][:, lanes], w2[u][:, lanes]], axis=1),
                     jnp.concatenate([jnp.zeros((C, 2 * HEAD), F32), vv[u][:, lanes]], axis=1)],
                    axis=0).astype(BF16)
                zero = jnp.zeros((), BF16)
                rhs = jnp.concatenate([jnp.where(pair_head == 0, blk, zero),
                                       jnp.where(pair_head == 1, blk, zero)], axis=0)
                outs.append(_mm(lhs, rhs))
            g_sbs.append(jnp.concatenate([o_[:, :2 * HEAD] for o_ in outs], axis=1)
                         + eye_sbs * pcu[u])
            h_sbs.append(jnp.concatenate([o_[:, 2 * HEAD:] for o_ in outs], axis=1))
        yield

        for cw_i in range(WAVE_CHUNKS):
            o_cols = []
            for gi in range(n_groups):
                u = cw_i * n_groups + gi
                qs = _mm(jnp.concatenate([q[u], g_sbs[u]], axis=0), bd(state[gi]))
                o_cols.append(qs[:C] + o_loc[u])
                state[gi] = qs[C:] + h_sbs[u]
            o_rows.append(jnp.concatenate(o_cols, axis=1))
            yield
    for gi in range(n_groups):
        s_ref[gi] = state[gi]
    o = jnp.concatenate(o_rows, axis=0)

    inv_n = 1.0 / HEAD
    mean = _mm(o, seg) * inv_n
    d = o - mean
    var = _mm(d * d, seg) * inv_n
    on = d * lax.rsqrt(var + GN_EPS) * w["gn_w"][...] + w["gn_b"][...]
    bonus = _mm(r * k2 * w["r_k"][...], seg) * v
    y_rwkv = (on + bonus) * g
    yield

    y = jnp.concatenate([y_conv, y_rwkv], axis=1)
    result["x1"] = x + _mm(y, w["w_out"][...])


def _channel_stream(x, p, w, uhist, result, *, ts, dff, final):
    h = _rms(x, w["g_ffn"][...]).astype(BF16)
    e = _rms(_mm(p, w["w_proj"][...]), w["g_ple"][...])
    n_ff = dff // FF_CHUNK

    def up(j):
        return [jnp.dot(h, w["ffn_up"][:, half * dff + j * FF_CHUNK:
                                       half * dff + (j + 1) * FF_CHUNK],
                        preferred_element_type=F32) for half in range(2)]

    def conv(u, cols):
        u1, u2 = _delayed(uhist, cols.start // LANES, u, (1, 2))
        return (u * w["ffn_cw"][2:3, cols] + u1 * w["ffn_cw"][1:2, cols]
                + u2 * w["ffn_cw"][0:1, cols] + w["ffn_cb"][:, cols])

    def activation(j, u):
        gate, val = [conv(u[half], slice(half * dff + j * FF_CHUNK,
                                         half * dff + (j + 1) * FF_CHUNK)) for half in range(2)]
        return (gate * _sigmoid(gate) * val).astype(BF16)

    acc = x
    u_next = up(0)
    yield
    for j in range(n_ff):
        u = u_next
        if j + 1 < n_ff:
            u_next = up(j + 1)
            yield
        act = activation(j, u)
        yield
        acc = acc + jnp.dot(act, w["ffn_down"][j * FF_CHUNK:(j + 1) * FF_CHUNK, :],
                            preferred_element_type=F32)
        yield
    x2 = acc

    gate = _sigmoid(_mm(_rms(x2, w["g_gate"][...]), w["w_gate"][...]))
    x3 = x2 + gate * e
    result["out"] = _rms(x3, w["g_fin"][...]) if final else x3


def _run_interleaved(gen_a, n_a, gen_b, n_b, ahead):
    done_b = 0
    for ka in range(n_a):
        next(gen_a, None)
        want_b = min(n_b, -(-((ka + 1 + ahead) * n_b) // n_a))
        while done_b < want_b:
            next(gen_b, None)
            done_b += 1
    for gen in (gen_a, gen_b):
        for _ in gen:
            pass


def _layer_kernel(x_ref, p_ref, *refs, names, ts, cw, rw, dff, n_s, n_tiles, final):
    n_w = len(names)
    w = dict(zip(names, refs[:n_w]))
    o_ref = refs[n_w]
    zhist, cxhist, s_ref, uhist, x1buf = refs[n_w + 1:]

    i = pl.program_id(0)
    mix_s = lax.rem(jnp.minimum(i, n_tiles - 1), jnp.int32(n_s))
    ch_s = lax.rem(jnp.maximum(i - 1, 0), jnp.int32(n_s))

    @pl.when(i == 0)
    def _():
        x1buf[...] = jnp.zeros(x1buf.shape, F32)

    @pl.when(mix_s == 0)
    def _():
        zhist[:, 0:HALO, :] = jnp.zeros((zhist.shape[0], HALO, LANES), F32)
        cxhist[:, 0:HALO, :] = jnp.zeros((cxhist.shape[0], HALO, LANES), F32)
        s_ref[...] = jnp.zeros(s_ref.shape, F32)

    @pl.when(ch_s == 0)
    def _():
        uhist[:, 0:HALO, :] = jnp.zeros((uhist.shape[0], HALO, LANES), F32)

    slot = lax.rem(i, jnp.int32(2))
    result = {}
    mixer = _mixer_stream(x_ref[0], w, zhist, cxhist, s_ref, result, ts=ts, cw=cw, rw=rw)
    channel = _channel_stream(x1buf[1 - slot], p_ref[0], w, uhist, result,
                              ts=ts, dff=dff, final=final)
    n_waves = ts // (WKV_CHUNK * WAVE_CHUNKS)
    n_mixer = 10 + n_waves * (5 + int(math.log2(WKV_CHUNK)) + WAVE_CHUNKS)
    n_channel = 1 + 3 * (dff // FF_CHUNK)
    _run_interleaved(mixer, n_mixer, channel, n_channel, CHANNEL_AHEAD)
    x1buf[slot] = result["x1"]
    o_ref[0] = result["out"]


def _const_spec(shape):
    nd = len(shape)
    return pl.BlockSpec(shape, lambda i: (0,) * nd, pipeline_mode=pl.Buffered(1))


def _row(vec):
    return vec.reshape(1, -1).astype(F32)


def kernel(x, p, mix_norm_g, w_in, conv_mix_w, rwkv_mu, rwkv_w0, rwkv_w_up, rwkv_a0, rwkv_a_up, rwkv_g_up, rwkv_k_k, rwkv_k_a, rwkv_r_k, rwkv_gn_w, rwkv_gn_b, w_out, ffn_norm_g, ffn_w_up, ffn_conv_w, ffn_conv_b, ffn_w_down, ple_w_proj, ple_norm_g, ple_gate_norm_g, ple_w_gate, final_norm_g):
    depth = w_in.shape[0]
    bsz, seq, d = x.shape
    cw = conv_mix_w.shape[-1]
    rw = rwkv_w0.shape[-1]
    dff = ffn_w_down.shape[1]
    n_dec, n_aicl, n_gate = rwkv_w_up.shape[1], rwkv_a_up.shape[1], rwkv_g_up.shape[1]
    ts = SEQ_TILE
    assert seq % ts == 0 and ts % WKV_CHUNK == 0 and rw % GROUP == 0 and dff % FF_CHUNK == 0
    assert n_dec + n_aicl == 128 and n_gate == 128 and rwkv_r_k.shape[-1] == HEAD
    n_s = seq // ts
    n_tiles = bsz * n_s

    def mixer_tile(i):
        t = jnp.minimum(i, n_tiles - 1)
        return (t // n_s, t % n_s, 0)

    def channel_tile(i):
        t = jnp.maximum(i - 1, 0)
        return (t // n_s, t % n_s, 0)

    for li in range(depth):
        wup_pad = jnp.concatenate([rwkv_w_up[li], jnp.zeros((n_aicl, rw), F32)], axis=0)
        aup_pad = jnp.concatenate([jnp.zeros((n_dec, rw), F32), rwkv_a_up[li]], axis=0)
        weights = {
            "g_mix": _row(mix_norm_g[li]), "w_in": w_in[li].astype(BF16),
            "conv_w": conv_mix_w[li].astype(F32), "mu": _row(rwkv_mu[li]),
            "w0": _row(rwkv_w0[li]), "w_up": wup_pad.astype(BF16), "a0": _row(rwkv_a0[li]),
            "a_up": aup_pad.astype(BF16), "g_up": rwkv_g_up[li].astype(BF16),
            "k_k": _row(rwkv_k_k[li]), "k_a": _row(rwkv_k_a[li]), "r_k": _row(rwkv_r_k[li]),
            "gn_w": _row(rwkv_gn_w[li]), "gn_b": _row(rwkv_gn_b[li]),
            "w_out": w_out[li].astype(BF16),
            "g_ffn": _row(ffn_norm_g[li]), "ffn_up": ffn_w_up[li].astype(BF16),
            "ffn_cw": ffn_conv_w[li].astype(F32), "ffn_cb": _row(ffn_conv_b[li]),
            "ffn_down": ffn_w_down[li].astype(BF16), "w_proj": ple_w_proj[li].astype(BF16),
            "g_ple": _row(ple_norm_g[li]), "g_gate": _row(ple_gate_norm_g[li]),
            "w_gate": ple_w_gate[li].astype(BF16), "g_fin": _row(final_norm_g),
        }
        names = tuple(weights)
        x = pl.pallas_call(
            functools.partial(_layer_kernel, names=names, ts=ts, cw=cw, rw=rw, dff=dff,
                              n_s=n_s, n_tiles=n_tiles, final=(li == depth - 1)),
            grid=(n_tiles + 1,),
            in_specs=[pl.BlockSpec((1, ts, d), mixer_tile),
                      pl.BlockSpec((1, ts, p.shape[-1]), channel_tile)]
            + [_const_spec(weights[n].shape) for n in names],
            out_specs=pl.BlockSpec((1, ts, d), channel_tile),
            out_shape=jax.ShapeDtypeStruct(x.shape, F32),
            scratch_shapes=[
                pltpu.VMEM(((w_in.shape[-1] - 3 * cw) // LANES, HALO + ts, LANES), F32),
                pltpu.VMEM((cw // LANES, HALO + ts, LANES), F32),
                pltpu.VMEM((rw // GROUP, HEAD, GROUP), F32),
                pltpu.VMEM((2 * dff // LANES, HALO + ts, LANES), F32),
                pltpu.VMEM((2, ts, d), F32),
            ],
            compiler_params=pltpu.CompilerParams(dimension_semantics=("arbitrary",),
                                                 vmem_limit_bytes=VMEM_LIMIT_BYTES),
            name="layer",
        )(x, p[li], *[weights[n] for n in names])
    return x
```

```python
import functools
import math

import jax
import jax.numpy as jnp
from jax import lax
from jax.experimental import pallas as pl
from jax.experimental.pallas import tpu as pltpu

NORM_EPS = 1e-6
GN_EPS = 64e-5
HEAD = 64
WKV_CHUNK = 64
GROUP = 256
HEADS_PER_GROUP = GROUP // HEAD
HALO = 8
LANES = 128
SEQ_TILE = 256
FF_CHUNK = 256
VMEM_LIMIT_BYTES = 56 * 1024 * 1024
WAVE_CHUNKS = 4
CHANNEL_AHEAD = 4

F32 = jnp.float32
BF16 = jnp.bfloat16


def _rms(x, g):
    ms = jnp.mean(x * x, axis=-1, keepdims=True)
    return x * lax.rsqrt(ms + NORM_EPS) * g


def _sigmoid(x):
    return 1.0 / (1.0 + jnp.exp(-x))


def _mm(a, b):
    return jnp.dot(a.astype(BF16), b.astype(BF16), preferred_element_type=F32)


def _mm_split2(a_bf16, b):
    b1 = b.astype(BF16)
    b2 = (b - b1.astype(F32)).astype(BF16)
    return (jnp.dot(a_bf16, b1, preferred_element_type=F32)
            + jnp.dot(a_bf16, b2, preferred_element_type=F32))


def _mm_nt(a, b):
    return lax.dot_general(a.astype(BF16), b.astype(BF16), (((1,), (1,)), ((), ())),
                           preferred_element_type=F32)


def _iota2(shape, dim):
    return lax.broadcasted_iota(jnp.int32, shape, dim)


def _delayed(hist, tile0, cur, delays):
    ts = cur.shape[0]
    n = cur.shape[1] // LANES
    for c in range(n):
        hist[tile0 + c, HALO:HALO + ts, :] = cur[:, c * LANES:(c + 1) * LANES]
    outs = [jnp.concatenate([hist[tile0 + c, HALO - d:HALO - d + ts, :] for c in range(n)], axis=1)
            for d in delays]
    for c in range(n):
        hist[tile0 + c, 0:HALO, :] = hist[tile0 + c, ts:ts + HALO, :]
    return outs


def _mixer_stream(x, w, zhist, cxhist, s_ref, result, *, ts, cw, rw):
    n_chunks = ts // WKV_CHUNK
    n_groups = rw // GROUP
    C = WKV_CHUNK

    h = _rms(x, w["g_mix"][...]).astype(BF16)

    zc = jnp.dot(h, w["w_in"][:, 0:3 * cw], preferred_element_type=F32)
    x_in = zc[:, 0:cw]
    b_gate = zc[:, cw:2 * cw]
    c_gate = zc[:, 2 * cw:3 * cw]
    cx = c_gate * x_in
    cx1, cx2 = _delayed(cxhist, 0, cx, (1, 2))
    conv = cx * w["conv_w"][2:3, :] + cx1 * w["conv_w"][1:2, :] + cx2 * w["conv_w"][0:1, :]
    y_conv = b_gate * conv
    yield

    zr = jnp.dot(h, w["w_in"][:, 3 * cw:], preferred_element_type=F32)
    yield
    (zs,) = _delayed(zhist, 0, zr, (1,))
    zz = zr + (zs - zr) * w["mu"][...]
    yield
    r = zz[:, 0:rw]
    k = zz[:, rw:2 * rw]
    v = zz[:, 2 * rw:3 * rw]
    lora_in = zz[:, 3 * rw:3 * rw + 128]
    gd = zz[:, 3 * rw + 128:3 * rw + 256]

    u_dec = w["w0"][...] + _mm(jnp.tanh(lora_in), w["w_up"][...])
    logw = (-math.exp(-0.5)) * _sigmoid(u_dec)
    a = _sigmoid(w["a0"][...] + _mm(lora_in, w["a_up"][...]))
    g = _mm(_sigmoid(gd), w["g_up"][...])
    yield

    seg = (_iota2((GROUP, GROUP), 0) // HEAD == _iota2((GROUP, GROUP), 1) // HEAD).astype(BF16)

    def head_sums(t):
        return jnp.concatenate([_mm(t[:, gi * GROUP:(gi + 1) * GROUP], seg)
                                for gi in range(n_groups)], axis=1)

    kk = k * w["k_k"][...]
    kk = kk * lax.rsqrt(jnp.maximum(head_sums(kk * kk), 1e-24))
    k2 = k * (1.0 + (a - 1.0) * w["k_a"][...])
    aa = -kk
    bb = kk * a
    yield

    ri = _iota2((ts, ts), 0)
    ci = _iota2((ts, ts), 1)
    tri = jnp.where(((ri // C) == (ci // C)) & (ci <= ri), 1.0, 0.0).astype(BF16)
    cl = _mm_split2(tri, logw)
    yield
    ec = jnp.exp(cl)
    eci = 1.0 / ec
    ecm = jnp.exp(cl - logw)
    pc = jnp.concatenate(
        [jnp.broadcast_to(ec[(c + 1) * C - 1:(c + 1) * C, :], (C, rw)) for c in range(n_chunks)],
        axis=0)
    yield
    a_t = aa * ecm
    r_t = r * ec
    b_t = bb * eci
    k_t = k2 * eci
    b_h = b_t * pc
    k_h = k_t * pc
    yield

    rs = _iota2((C, GROUP), 0)
    cs = _iota2((C, GROUP), 1) % HEAD
    lower_strict = cs < rs
    lower_incl = cs <= rs
    eye_sbs = jnp.where(cs == rs, 1.0, 0.0).astype(F32)
    rb = _iota2((GROUP, GROUP), 0)
    cb = _iota2((GROUP, GROUP), 1)
    bd_mask = (rb // HEAD) == (cb // HEAD)
    pair_head = (_iota2((2 * C, GROUP), 1) % (2 * HEAD)) // HEAD

    def bd(y):
        yb = y.astype(BF16)
        return jnp.where(bd_mask, jnp.concatenate([yb] * HEADS_PER_GROUP, axis=0),
                         jnp.zeros((), BF16))

    state = [s_ref[gi] for gi in range(n_groups)]
    o_rows = []
    for wave_start in range(0, n_chunks, WAVE_CHUNKS):
        units = [(c, gi) for c in range(wave_start, wave_start + WAVE_CHUNKS)
                 for gi in range(n_groups)]
        nu = len(units)

        def cut(arr):
            return [arr[c * C:(c + 1) * C, gi * GROUP:(gi + 1) * GROUP] for c, gi in units]

        at, rt, bt, kt, bh, kh, vv, pcu = (cut(a_t), cut(r_t), cut(b_t), cut(k_t), cut(b_h),
                                           cut(k_h), cut(v), cut(pc))
        ar = [jnp.concatenate([at[u], rt[u]], axis=0) for u in range(nu)]
        a1 = [_mm_nt(ar[u], bd(bt[u])) for u in range(nu)]
        a2 = [_mm_nt(ar[u], bd(kt[u])) for u in range(nu)]
        a_ab = [jnp.where(lower_strict, a1[u][:C], 0.0) for u in range(nu)]
        a_rb = [jnp.where(lower_incl, a1[u][C:], 0.0) for u in range(nu)]
        a_ak = [jnp.where(lower_strict, a2[u][:C], 0.0) for u in range(nu)]
        a_rk = [jnp.where(lower_incl, a2[u][C:], 0.0) for u in range(nu)]
        yield
        av = [_mm(jnp.concatenate([a_ak[u], a_rk[u]], axis=0), bd(vv[u])) for u in range(nu)]
        yield

        lp = a_ab
        tinv = [eye_sbs + a_ab[u] for u in range(nu)]
        n_sq = int(math.log2(C))
        for i in range(n_sq):
            m = [bd(lp[u]) for u in range(nu)]
            if i == 0:
                lp = [_mm(lp[u], m[u]) for u in range(nu)]
            elif i < n_sq - 1:
                st = [_mm(jnp.concatenate([lp[u], tinv[u]], axis=0), m[u]) for u in range(nu)]
                lp = [st[u][:C] for u in range(nu)]
                tinv = [tinv[u] + st[u][C:] for u in range(nu)]
            else:
                tinv = [tinv[u] + _mm(tinv[u], m[u]) for u in range(nu)]
            yield

        tm = [jnp.concatenate([tinv[u], _mm(a_rb[u], bd(tinv[u]))], axis=0) for u in range(nu)]
        yield
        s1 = [_mm(tm[u], bd(at[u])) for u in range(nu)]
        s2 = [_mm(tm[u], bd(av[u][:C])) for u in range(nu)]
        w1 = [s1[u][:C] for u in range(nu)]
        w2 = [s2[u][:C] for u in range(nu)]
        q = [rt[u] + s1[u][C:] for u in range(nu)]
        o_loc = [s2[u][C:] + av[u][C:] for u in range(nu)]
        yield
        g_sbs, h_sbs = [], []
        for u in range(nu):
            xt = jnp.concatenate([bh[u], kh[u]], axis=0).T
            outs = []
            for pair in range(HEADS_PER_GROUP // 2):
                lhs = jnp.concatenate([xt[(2 * pair) * HEAD:(2 * pair + 1) * HEAD],
                                       xt[(2 * pair + 1) * HEAD:(2 * pair + 2) * HEAD]], axis=1)
                lanes = slice(pair * 2 * HEAD, (pair + 1) * 2 * HEAD)
                blk = jnp.concatenate(
                    [jnp.concatenate([w1[u][:, lanes], w2[u][:, lanes]], axis=1),
                     jnp.concatenate([jnp.zeros((C, 2 * HEAD), F32), vv[u][:, lanes]], axis=1)],
                    axis=0).astype(BF16)
                zero = jnp.zeros((), BF16)
                rhs = jnp.concatenate([jnp.where(pair_head == 0, blk, zero),
                                       jnp.where(pair_head == 1, blk, zero)], axis=0)
                outs.append(_mm(lhs, rhs))
            g_sbs.append(jnp.concatenate([o_[:, :2 * HEAD] for o_ in outs], axis=1)
                         + eye_sbs * pcu[u])
            h_sbs.append(jnp.concatenate([o_[:, 2 * HEAD:] for o_ in outs], axis=1))
        yield

        for cw_i in range(WAVE_CHUNKS):
            o_cols = []
            for gi in range(n_groups):
                u = cw_i * n_groups + gi
                qs = _mm(jnp.concatenate([q[u], g_sbs[u]], axis=0), bd(state[gi]))
                o_cols.append(qs[:C] + o_loc[u])
                state[gi] = qs[C:] + h_sbs[u]
            o_rows.append(jnp.concatenate(o_cols, axis=1))
            yield
    for gi in range(n_groups):
        s_ref[gi] = state[gi]
    o = jnp.concatenate(o_rows, axis=0)

    inv_n = 1.0 / HEAD
    mean = head_sums(o) * inv_n
    d = o - mean
    var = head_sums(d * d) * inv_n
    on = d * lax.rsqrt(var + GN_EPS) * w["gn_w"][...] + w["gn_b"][...]
    bonus = head_sums(r * k2 * w["r_k"][...]) * v
    y_rwkv = (on + bonus) * g
    yield

    y = jnp.concatenate([y_conv, y_rwkv], axis=1)
    result["x1"] = x + _mm(y, w["w_out"][...])


def _channel_stream(x, p, w, uhist, result, *, ts, dff, final):
    h = _rms(x, w["g_ffn"][...]).astype(BF16)
    e = _rms(_mm(p, w["w_proj"][...]), w["g_ple"][...])
    n_ff = dff // FF_CHUNK

    def up(j):
        return [jnp.dot(h, w["ffn_up"][:, half * dff + j * FF_CHUNK:
                                       half * dff + (j + 1) * FF_CHUNK],
                        preferred_element_type=F32) for half in range(2)]

    def conv(u, cols):
        u1, u2 = _delayed(uhist, cols.start // LANES, u, (1, 2))
        return (u * w["ffn_cw"][2:3, cols] + u1 * w["ffn_cw"][1:2, cols]
                + u2 * w["ffn_cw"][0:1, cols] + w["ffn_cb"][:, cols])

    def activation(j, u):
        gate, val = [conv(u[half], slice(half * dff + j * FF_CHUNK,
                                         half * dff + (j + 1) * FF_CHUNK)) for half in range(2)]
        return (gate * _sigmoid(gate) * val).astype(BF16)

    acc = x
    u_next = up(0)
    yield
    for j in range(n_ff):
        u = u_next
        if j + 1 < n_ff:
            u_next = up(j + 1)
            yield
        act = activation(j, u)
        yield
        acc = acc + jnp.dot(act, w["ffn_down"][j * FF_CHUNK:(j + 1) * FF_CHUNK, :],
                            preferred_element_type=F32)
        yield
    x2 = acc

    gate = _sigmoid(_mm(_rms(x2, w["g_gate"][...]), w["w_gate"][...]))
    x3 = x2 + gate * e
    result["out"] = _rms(x3, w["g_fin"][...]) if final else x3


def _run_interleaved(gen_a, n_a, gen_b, n_b, ahead):
    done_b = 0
    for ka in range(n_a):
        next(gen_a, None)
        want_b = min(n_b, -(-((ka + 1 + ahead) * n_b) // n_a))
        while done_b < want_b:
            next(gen_b, None)
            done_b += 1
    for gen in (gen_a, gen_b):
        for _ in gen:
            pass


def _layer_kernel(x_ref, p_ref, *refs, names, ts, cw, rw, dff, n_s, n_tiles, final):
    n_w = len(names)
    w = dict(zip(names, refs[:n_w]))
    o_ref = refs[n_w]
    zhist, cxhist, s_ref, uhist, x1buf = refs[n_w + 1:]

    i = pl.program_id(0)
    mix_s = lax.rem(jnp.minimum(i, n_tiles - 1), jnp.int32(n_s))
    ch_s = lax.rem(jnp.maximum(i - 1, 0), jnp.int32(n_s))

    @pl.when(i == 0)
    def _():
        x1buf[...] = jnp.zeros(x1buf.shape, F32)

    @pl.when(mix_s == 0)
    def _():
        zhist[:, 0:HALO, :] = jnp.zeros((zhist.shape[0], HALO, LANES), F32)
        cxhist[:, 0:HALO, :] = jnp.zeros((cxhist.shape[0], HALO, LANES), F32)
        s_ref[...] = jnp.zeros(s_ref.shape, F32)

    @pl.when(ch_s == 0)
    def _():
        uhist[:, 0:HALO, :] = jnp.zeros((uhist.shape[0], HALO, LANES), F32)

    slot = lax.rem(i, jnp.int32(2))
    result = {}
    mixer = _mixer_stream(x_ref[0], w, zhist, cxhist, s_ref, result, ts=ts, cw=cw, rw=rw)
    channel = _channel_stream(x1buf[1 - slot], p_ref[0], w, uhist, result,
                              ts=ts, dff=dff, final=final)
    n_waves = ts // (WKV_CHUNK * WAVE_CHUNKS)
    n_mixer = 10 + n_waves * (5 + int(math.log2(WKV_CHUNK)) + WAVE_CHUNKS)
    n_channel = 1 + 3 * (dff // FF_CHUNK)
    _run_interleaved(mixer, n_mixer, channel, n_channel, CHANNEL_AHEAD)
    x1buf[slot] = result["x1"]
    o_ref[0] = result["out"]


def _const_spec(shape):
    nd = len(shape)
    return pl.BlockSpec(shape, lambda i: (0,) * nd, pipeline_mode=pl.Buffered(1))


def _row(vec):
    return vec.reshape(1, -1).astype(F32)


def kernel(x, p, mix_norm_g, w_in, conv_mix_w, rwkv_mu, rwkv_w0, rwkv_w_up, rwkv_a0, rwkv_a_up, rwkv_g_up, rwkv_k_k, rwkv_k_a, rwkv_r_k, rwkv_gn_w, rwkv_gn_b, w_out, ffn_norm_g, ffn_w_up, ffn_conv_w, ffn_conv_b, ffn_w_down, ple_w_proj, ple_norm_g, ple_gate_norm_g, ple_w_gate, final_norm_g):
    depth = w_in.shape[0]
    bsz, seq, d = x.shape
    cw = conv_mix_w.shape[-1]
    rw = rwkv_w0.shape[-1]
    dff = ffn_w_down.shape[1]
    n_dec, n_aicl, n_gate = rwkv_w_up.shape[1], rwkv_a_up.shape[1], rwkv_g_up.shape[1]
    ts = SEQ_TILE
    assert seq % ts == 0 and ts % WKV_CHUNK == 0 and rw % GROUP == 0 and dff % FF_CHUNK == 0
    assert n_dec + n_aicl == 128 and n_gate == 128 and rwkv_r_k.shape[-1] == HEAD
    n_s = seq // ts
    n_tiles = bsz * n_s

    def mixer_tile(i):
        t = jnp.minimum(i, n_tiles - 1)
        return (t // n_s, t % n_s, 0)

    def channel_tile(i):
        t = jnp.maximum(i - 1, 0)
        return (t // n_s, t % n_s, 0)

    for li in range(depth):
        wup_pad = jnp.concatenate([rwkv_w_up[li], jnp.zeros((n_aicl, rw), F32)], axis=0)
        aup_pad = jnp.concatenate([jnp.zeros((n_dec, rw), F32), rwkv_a_up[li]], axis=0)
        weights = {
            "g_mix": _row(mix_norm_g[li]), "w_in": w_in[li].astype(BF16),
            "conv_w": conv_mix_w[li].astype(F32), "mu": _row(rwkv_mu[li]),
            "w0": _row(rwkv_w0[li]), "w_up": wup_pad.astype(BF16), "a0": _row(rwkv_a0[li]),
            "a_up": aup_pad.astype(BF16), "g_up": rwkv_g_up[li].astype(BF16),
            "k_k": _row(rwkv_k_k[li]), "k_a": _row(rwkv_k_a[li]), "r_k": _row(rwkv_r_k[li]),
            "gn_w": _row(rwkv_gn_w[li]), "gn_b": _row(rwkv_gn_b[li]),
            "w_out": w_out[li].astype(BF16),
            "g_ffn": _row(ffn_norm_g[li]), "ffn_up": ffn_w_up[li].astype(BF16),
            "ffn_cw": ffn_conv_w[li].astype(F32), "ffn_cb": _row(ffn_conv_b[li]),
            "ffn_down": ffn_w_down[li].astype(BF16), "w_proj": ple_w_proj[li].astype(BF16),
            "g_ple": _row(ple_norm_g[li]), "g_gate": _row(ple_gate_norm_g[li]),
            "w_gate": ple_w_gate[li].astype(BF16), "g_fin": _row(final_norm_g),
        }
        names = tuple(weights)
        x = pl.pallas_call(
            functools.partial(_layer_kernel, names=names, ts=ts, cw=cw, rw=rw, dff=dff,
                              n_s=n_s, n_tiles=n_tiles, final=(li == depth - 1)),
            grid=(n_tiles + 1,),
            in_specs=[pl.BlockSpec((1, ts, d), mixer_tile),
                      pl.BlockSpec((1, ts, p.shape[-1]), channel_tile)]
            + [_const_spec(weights[n].shape) for n in names],
            out_specs=pl.BlockSpec((1, ts, d), channel_tile),
            out_shape=jax.ShapeDtypeStruct(x.shape, F32),
            scratch_shapes=[
                pltpu.VMEM(((w_in.shape[-1] - 3 * cw) // LANES, HALO + ts, LANES), F32),
                pltpu.VMEM((cw // LANES, HALO + ts, LANES), F32),
                pltpu.VMEM((rw // GROUP, HEAD, GROUP), F32),
                pltpu.VMEM((2 * dff // LANES, HALO + ts, LANES), F32),
                pltpu.VMEM((2, ts, d), F32),
            ],
            compiler_params=pltpu.CompilerParams(dimension_semantics=("arbitrary",),
                                                 vmem_limit_bytes=VMEM_LIMIT_BYTES),
            name="layer",
        )(x, p[li], *[weights[n] for n in names])
    return x
```

```python
import functools
import math

import jax
import jax.numpy as jnp
from jax import lax
from jax.experimental import pallas as pl
from jax.experimental.pallas import tpu as pltpu

NORM_EPS = 1e-6
GN_EPS = 64e-5
HEAD = 64
WKV_CHUNK = 64
GROUP = 256
HEADS_PER_GROUP = GROUP // HEAD
HALO = 8
LANES = 128
SEQ_TILE = 256
FF_CHUNK = 256
VMEM_LIMIT_BYTES = 56 * 1024 * 1024
WAVE_CHUNKS = 4
CHANNEL_AHEAD = 4

F32 = jnp.float32
BF16 = jnp.bfloat16


def _rms(x, g):
    ms = jnp.mean(x * x, axis=-1, keepdims=True)
    return x * lax.rsqrt(ms + NORM_EPS) * g


def _sigmoid(x):
    return 1.0 / (1.0 + jnp.exp(-x))


def _mm(a, b):
    return jnp.dot(a.astype(BF16), b.astype(BF16), preferred_element_type=F32)


def _mm_split2(a_bf16, b):
    b1 = b.astype(BF16)
    b2 = (b - b1.astype(F32)).astype(BF16)
    return (jnp.dot(a_bf16, b1, preferred_element_type=F32)
            + jnp.dot(a_bf16, b2, preferred_element_type=F32))


def _mm_nt(a, b):
    return lax.dot_general(a.astype(BF16), b.astype(BF16), (((1,), (1,)), ((), ())),
                           preferred_element_type=F32)


def _iota2(shape, dim):
    return lax.broadcasted_iota(jnp.int32, shape, dim)


def _hist_put(hist, tile0, cur):
    ts = cur.shape[0]
    for c in range(cur.shape[1] // LANES):
        hist[tile0 + c, HALO:HALO + ts, :] = cur[:, c * LANES:(c + 1) * LANES]


def _hist_get(hist, tile0, n, ts, delays):
    outs = [jnp.concatenate([hist[tile0 + c, HALO - d:HALO - d + ts, :] for c in range(n)], axis=1)
            for d in delays]
    for c in range(n):
        hist[tile0 + c, 0:HALO, :] = hist[tile0 + c, ts:ts + HALO, :]
    return outs


def _mixer_stream(x, w, zhist, cxhist, s_ref, result, *, ts, cw, rw):
    n_chunks = ts // WKV_CHUNK
    n_groups = rw // GROUP
    C = WKV_CHUNK

    h = _rms(x, w["g_mix"][...]).astype(BF16)

    zc = jnp.dot(h, w["w_in"][:, 0:3 * cw], preferred_element_type=F32)
    x_in = zc[:, 0:cw]
    b_gate = zc[:, cw:2 * cw]
    c_gate = zc[:, 2 * cw:3 * cw]
    cx = c_gate * x_in
    _hist_put(cxhist, 0, cx)
    yield

    zr = jnp.dot(h, w["w_in"][:, 3 * cw:], preferred_element_type=F32)
    _hist_put(zhist, 0, zr)
    yield
    cx1, cx2 = _hist_get(cxhist, 0, cw // LANES, ts, (1, 2))
    conv = cx * w["conv_w"][2:3, :] + cx1 * w["conv_w"][1:2, :] + cx2 * w["conv_w"][0:1, :]
    y_conv = b_gate * conv
    (zs,) = _hist_get(zhist, 0, zr.shape[1] // LANES, ts, (1,))
    zz = zr + (zs - zr) * w["mu"][...]
    yield
    r = zz[:, 0:rw]
    k = zz[:, rw:2 * rw]
    v = zz[:, 2 * rw:3 * rw]
    lora_in = zz[:, 3 * rw:3 * rw + 128]
    gd = zz[:, 3 * rw + 128:3 * rw + 256]

    u_dec = w["w0"][...] + _mm(jnp.tanh(lora_in), w["w_up"][...])
    logw = (-math.exp(-0.5)) * _sigmoid(u_dec)
    a = _sigmoid(w["a0"][...] + _mm(lora_in, w["a_up"][...]))
    g = _mm(_sigmoid(gd), w["g_up"][...])
    yield

    seg = (_iota2((rw, rw), 0) // HEAD == _iota2((rw, rw), 1) // HEAD).astype(BF16)

    def head_sums(t):
        return _mm(t, seg)

    kk = k * w["k_k"][...]
    kk = kk * lax.rsqrt(jnp.maximum(head_sums(kk * kk), 1e-24))
    k2 = k * (1.0 + (a - 1.0) * w["k_a"][...])
    aa = -kk
    bb = kk * a
    yield

    ri = _iota2((ts, ts), 0)
    ci = _iota2((ts, ts), 1)
    tri = jnp.where(((ri // C) == (ci // C)) & (ci <= ri), 1.0, 0.0).astype(BF16)
    cl = _mm_split2(tri, logw)
    yield
    ec = jnp.exp(cl)
    eci = 1.0 / ec
    ecm = jnp.exp(cl - logw)
    pc = jnp.concatenate(
        [jnp.broadcast_to(ec[(c + 1) * C - 1:(c + 1) * C, :], (C, rw)) for c in range(n_chunks)],
        axis=0)
    yield
    a_t = aa * ecm
    r_t = r * ec
    b_t = bb * eci
    k_t = k2 * eci
    b_h = b_t * pc
    k_h = k_t * pc
    yield

    rs = _iota2((C, GROUP), 0)
    cs = _iota2((C, GROUP), 1) % HEAD
    lower_strict = cs < rs
    lower_incl = cs <= rs
    eye_sbs = jnp.where(cs == rs, 1.0, 0.0).astype(F32)
    rb = _iota2((GROUP, GROUP), 0)
    cb = _iota2((GROUP, GROUP), 1)
    bd_mask = (rb // HEAD) == (cb // HEAD)
    pair_head = (_iota2((2 * C, GROUP), 1) % (2 * HEAD)) // HEAD

    def bd(y):
        yb = y.astype(BF16)
        return jnp.where(bd_mask, jnp.concatenate([yb] * HEADS_PER_GROUP, axis=0),
                         jnp.zeros((), BF16))

    state = [s_ref[gi] for gi in range(n_groups)]
    o_rows = []
    for wave_start in range(0, n_chunks, WAVE_CHUNKS):
        units = [(c, gi) for c in range(wave_start, wave_start + WAVE_CHUNKS)
                 for gi in range(n_groups)]
        nu = len(units)

        def cut(arr):
            return [arr[c * C:(c + 1) * C, gi * GROUP:(gi + 1) * GROUP] for c, gi in units]

        at, rt, bt, kt, bh, kh, vv, pcu = (cut(a_t), cut(r_t), cut(b_t), cut(k_t), cut(b_h),
                                           cut(k_h), cut(v), cut(pc))
        ar = [jnp.concatenate([at[u], rt[u]], axis=0) for u in range(nu)]
        a1 = [_mm_nt(ar[u], bd(bt[u])) for u in range(nu)]
        a2 = [_mm_nt(ar[u], bd(kt[u])) for u in range(nu)]
        a_ab = [jnp.where(lower_strict, a1[u][:C], 0.0) for u in range(nu)]
        a_rb = [jnp.where(lower_incl, a1[u][C:], 0.0) for u in range(nu)]
        a_ak = [jnp.where(lower_strict, a2[u][:C], 0.0) for u in range(nu)]
        a_rk = [jnp.where(lower_incl, a2[u][C:], 0.0) for u in range(nu)]
        yield
        av = [_mm(jnp.concatenate([a_ak[u], a_rk[u]], axis=0), bd(vv[u])) for u in range(nu)]
        yield

        lp = a_ab
        tinv = [eye_sbs + a_ab[u] for u in range(nu)]
        n_sq = int(math.log2(C))
        for i in range(n_sq):
            m = [bd(lp[u]) for u in range(nu)]
            if i == 0:
                lp = [_mm(lp[u], m[u]) for u in range(nu)]
            elif i < n_sq - 1:
                st = [_mm(jnp.concatenate([lp[u], tinv[u]], axis=0), m[u]) for u in range(nu)]
                lp = [st[u][:C] for u in range(nu)]
                tinv = [tinv[u] + st[u][C:] for u in range(nu)]
            else:
                tinv = [tinv[u] + _mm(tinv[u], m[u]) for u in range(nu)]
            yield

        tm = [jnp.concatenate([tinv[u], _mm(a_rb[u], bd(tinv[u]))], axis=0) for u in range(nu)]
        yield
        s1 = [_mm(tm[u], bd(at[u])) for u in range(nu)]
        s2 = [_mm(tm[u], bd(av[u][:C])) for u in range(nu)]
        w1 = [s1[u][:C] for u in range(nu)]
        w2 = [s2[u][:C] for u in range(nu)]
        q = [rt[u] + s1[u][C:] for u in range(nu)]
        o_loc = [s2[u][C:] + av[u][C:] for u in range(nu)]
        yield
        g_sbs, h_sbs = [], []
        for u in range(nu):
            xt = jnp.concatenate([bh[u], kh[u]], axis=0).T
            outs = []
            for pair in range(HEADS_PER_GROUP // 2):
                lhs = jnp.concatenate([xt[(2 * pair) * HEAD:(2 * pair + 1) * HEAD],
                                       xt[(2 * pair + 1) * HEAD:(2 * pair + 2) * HEAD]], axis=1)
                lanes = slice(pair * 2 * HEAD, (pair + 1) * 2 * HEAD)
                blk = jnp.concatenate(
                    [jnp.concatenate([w1[u][:, lanes], w2[u][:, lanes]], axis=1),
                     jnp.concatenate([jnp.zeros((C, 2 * HEAD), F32), vv[u][:, lanes]], axis=1)],
                    axis=0).astype(BF16)
                zero = jnp.zeros((), BF16)
                rhs = jnp.concatenate([jnp.where(pair_head == 0, blk, zero),
                                       jnp.where(pair_head == 1, blk, zero)], axis=0)
                outs.append(_mm(lhs, rhs))
            g_sbs.append(jnp.concatenate([o_[:, :2 * HEAD] for o_ in outs], axis=1)
                         + eye_sbs * pcu[u])
            h_sbs.append(jnp.concatenate([o_[:, 2 * HEAD:] for o_ in outs], axis=1))
        yield

        for cw_i in range(WAVE_CHUNKS):
            o_cols = []
            for gi in range(n_groups):
                u = cw_i * n_groups + gi
                qs = _mm(jnp.concatenate([q[u], g_sbs[u]], axis=0), bd(state[gi]))
                o_cols.append(qs[:C] + o_loc[u])
                state[gi] = qs[C:] + h_sbs[u]
            o_rows.append(jnp.concatenate(o_cols, axis=1))
            yield
    for gi in range(n_groups):
        s_ref[gi] = state[gi]
    o = jnp.concatenate(o_rows, axis=0)

    inv_n = 1.0 / HEAD
    mean = head_sums(o) * inv_n
    d = o - mean
    var = head_sums(d * d) * inv_n
    on = d * lax.rsqrt(var + GN_EPS) * w["gn_w"][...] + w["gn_b"][...]
    bonus = head_sums(r * k2 * w["r_k"][...]) * v
    y_rwkv = (on + bonus) * g
    yield

    y = jnp.concatenate([y_conv, y_rwkv], axis=1)
    result["x1"] = x + _mm(y, w["w_out"][...])


def _channel_stream(x, p, w, uhist, result, *, ts, dff, final):
    h = _rms(x, w["g_ffn"][...]).astype(BF16)
    e = _rms(_mm(p, w["w_proj"][...]), w["g_ple"][...])
    n_ff = dff // FF_CHUNK

    def up(j):
        us = []
        for half in range(2):
            lo = half * dff + j * FF_CHUNK
            us.append(jnp.dot(h, w["ffn_up"][:, lo:lo + FF_CHUNK], preferred_element_type=F32))
            _hist_put(uhist, lo // LANES, us[-1])
        return us

    def conv(u, cols):
        u1, u2 = _hist_get(uhist, cols.start // LANES, FF_CHUNK // LANES, ts, (1, 2))
        return (u * w["ffn_cw"][2:3, cols] + u1 * w["ffn_cw"][1:2, cols]
                + u2 * w["ffn_cw"][0:1, cols] + w["ffn_cb"][:, cols])

    def activation(j, u):
        gate, val = [conv(u[half], slice(half * dff + j * FF_CHUNK,
                                         half * dff + (j + 1) * FF_CHUNK)) for half in range(2)]
        return (gate * _sigmoid(gate) * val).astype(BF16)

    acc = x
    u_next = up(0)
    yield
    for j in range(n_ff):
        u = u_next
        if j + 1 < n_ff:
            u_next = up(j + 1)
            yield
        act = activation(j, u)
        yield
        acc = acc + jnp.dot(act, w["ffn_down"][j * FF_CHUNK:(j + 1) * FF_CHUNK, :],
                            preferred_element_type=F32)
        yield
    x2 = acc

    gate = _sigmoid(_mm(_rms(x2, w["g_gate"][...]), w["w_gate"][...]))
    x3 = x2 + gate * e
    result["out"] = _rms(x3, w["g_fin"][...]) if final else x3


def _run_interleaved(gen_a, n_a, gen_b, n_b, ahead):
    done_b = 0
    for ka in range(n_a):
        next(gen_a, None)
        want_b = min(n_b, -(-((ka + 1 + ahead) * n_b) // n_a))
        while done_b < want_b:
            next(gen_b, None)
            done_b += 1
    for gen in (gen_a, gen_b):
        for _ in gen:
            pass


def _layer_kernel(x_ref, p_ref, *refs, names, ts, cw, rw, dff, n_s, n_tiles, final):
    n_w = len(names)
    w = dict(zip(names, refs[:n_w]))
    o_ref = refs[n_w]
    zhist, cxhist, s_ref, uhist, x1buf = refs[n_w + 1:]

    i = pl.program_id(0)
    mix_s = lax.rem(jnp.minimum(i, n_tiles - 1), jnp.int32(n_s))
    ch_s = lax.rem(jnp.maximum(i - 1, 0), jnp.int32(n_s))

    @pl.when(i == 0)
    def _():
        x1buf[...] = jnp.zeros(x1buf.shape, F32)

    @pl.when(mix_s == 0)
    def _():
        zhist[:, 0:HALO, :] = jnp.zeros((zhist.shape[0], HALO, LANES), F32)
        cxhist[:, 0:HALO, :] = jnp.zeros((cxhist.shape[0], HALO, LANES), F32)
        s_ref[...] = jnp.zeros(s_ref.shape, F32)

    @pl.when(ch_s == 0)
    def _():
        uhist[:, 0:HALO, :] = jnp.zeros((uhist.shape[0], HALO, LANES), F32)

    slot = lax.rem(i, jnp.int32(2))
    result = {}
    mixer = _mixer_stream(x_ref[0], w, zhist, cxhist, s_ref, result, ts=ts, cw=cw, rw=rw)
    channel = _channel_stream(x1buf[1 - slot], p_ref[0], w, uhist, result,
                              ts=ts, dff=dff, final=final)
    n_waves = ts // (WKV_CHUNK * WAVE_CHUNKS)
    n_mixer = 10 + n_waves * (5 + int(math.log2(WKV_CHUNK)) + WAVE_CHUNKS)
    n_channel = 1 + 3 * (dff // FF_CHUNK)
    _run_interleaved(mixer, n_mixer, channel, n_channel, CHANNEL_AHEAD)
    x1buf[slot] = result["x1"]
    o_ref[0] = result["out"]


def _const_spec(shape):
    nd = len(shape)
    return pl.BlockSpec(shape, lambda i: (0,) * nd, pipeline_mode=pl.Buffered(1))


def _row(vec):
    return vec.reshape(1, -1).astype(F32)


def kernel(x, p, mix_norm_g, w_in, conv_mix_w, rwkv_mu, rwkv_w0, rwkv_w_up, rwkv_a0, rwkv_a_up, rwkv_g_up, rwkv_k_k, rwkv_k_a, rwkv_r_k, rwkv_gn_w, rwkv_gn_b, w_out, ffn_norm_g, ffn_w_up, ffn_conv_w, ffn_conv_b, ffn_w_down, ple_w_proj, ple_norm_g, ple_gate_norm_g, ple_w_gate, final_norm_g):
    depth = w_in.shape[0]
    bsz, seq, d = x.shape
    cw = conv_mix_w.shape[-1]
    rw = rwkv_w0.shape[-1]
    dff = ffn_w_down.shape[1]
    n_dec, n_aicl, n_gate = rwkv_w_up.shape[1], rwkv_a_up.shape[1], rwkv_g_up.shape[1]
    ts = SEQ_TILE
    assert seq % ts == 0 and ts % WKV_CHUNK == 0 and rw % GROUP == 0 and dff % FF_CHUNK == 0
    assert n_dec + n_aicl == 128 and n_gate == 128 and rwkv_r_k.shape[-1] == HEAD
    n_s = seq // ts
    n_tiles = bsz * n_s

    def mixer_tile(i):
        t = jnp.minimum(i, n_tiles - 1)
        return (t // n_s, t % n_s, 0)

    def channel_tile(i):
        t = jnp.maximum(i - 1, 0)
        return (t // n_s, t % n_s, 0)

    for li in range(depth):
        wup_pad = jnp.concatenate([rwkv_w_up[li], jnp.zeros((n_aicl, rw), F32)], axis=0)
        aup_pad = jnp.concatenate([jnp.zeros((n_dec, rw), F32), rwkv_a_up[li]], axis=0)
        weights = {
            "g_mix": _row(mix_norm_g[li]), "w_in": w_in[li].astype(BF16),
            "conv_w": conv_mix_w[li].astype(F32), "mu": _row(rwkv_mu[li]),
            "w0": _row(rwkv_w0[li]), "w_up": wup_pad.astype(BF16), "a0": _row(rwkv_a0[li]),
            "a_up": aup_pad.astype(BF16), "g_up": rwkv_g_up[li].astype(BF16),
            "k_k": _row(rwkv_k_k[li]), "k_a": _row(rwkv_k_a[li]), "r_k": _row(rwkv_r_k[li]),
            "gn_w": _row(rwkv_gn_w[li]), "gn_b": _row(rwkv_gn_b[li]),
            "w_out": w_out[li].astype(BF16),
            "g_ffn": _row(ffn_norm_g[li]), "ffn_up": ffn_w_up[li].astype(BF16),
            "ffn_cw": ffn_conv_w[li].astype(F32), "ffn_cb": _row(ffn_conv_b[li]),
            "ffn_down": ffn_w_down[li].astype(BF16), "w_proj": ple_w_proj[li].astype(BF16),
            "g_ple": _row(ple_norm_g[li]), "g_gate": _row(ple_gate_norm_g[li]),
            "w_gate": ple_w_gate[li].astype(BF16), "g_fin": _row(final_norm_g),
        }
        names = tuple(weights)
        x = pl.pallas_call(
            functools.partial(_layer_kernel, names=names, ts=ts, cw=cw, rw=rw, dff=dff,
                              n_s=n_s, n_tiles=n_tiles, final=(li == depth - 1)),
            grid=(n_tiles + 1,),
            in_specs=[pl.BlockSpec((1, ts, d), mixer_tile),
                      pl.BlockSpec((1, ts, p.shape[-1]), channel_tile)]
            + [_const_spec(weights[n].shape) for n in names],
            out_specs=pl.BlockSpec((1, ts, d), channel_tile),
            out_shape=jax.ShapeDtypeStruct(x.shape, F32),
            scratch_shapes=[
                pltpu.VMEM(((w_in.shape[-1] - 3 * cw) // LANES, HALO + ts, LANES), F32),
                pltpu.VMEM((cw // LANES, HALO + ts, LANES), F32),
                pltpu.VMEM((rw // GROUP, HEAD, GROUP), F32),
                pltpu.VMEM((2 * dff // LANES, HALO + ts, LANES), F32),
                pltpu.VMEM((2, ts, d), F32),
            ],
            compiler_params=pltpu.CompilerParams(dimension_semantics=("arbitrary",),
                                                 vmem_limit_bytes=VMEM_LIMIT_BYTES),
            name="layer",
        )(x, p[li], *[weights[n] for n in names])
    return x
```

```python
import functools
import math

import jax
import jax.numpy as jnp
from jax import lax
from jax.experimental import pallas as pl
from jax.experimental.pallas import tpu as pltpu

NORM_EPS = 1e-6
GN_EPS = 64e-5
HEAD = 64
WKV_CHUNK = 64
GROUP = 256
HEADS_PER_GROUP = GROUP // HEAD
HALO = 8
LANES = 128
SEQ_TILE = 256
FF_CHUNK = 256
VMEM_LIMIT_BYTES = 56 * 1024 * 1024
WAVE_CHUNKS = 4
CHANNEL_AHEAD = 4

F32 = jnp.float32
BF16 = jnp.bfloat16


def _rms(x, g):
    ms = jnp.mean(x * x, axis=-1, keepdims=True)
    return x * lax.rsqrt(ms + NORM_EPS) * g


def _sigmoid(x):
    return 1.0 / (1.0 + jnp.exp(-x))


def _mm(a, b):
    return jnp.dot(a.astype(BF16), b.astype(BF16), preferred_element_type=F32)


def _mm_split2(a_bf16, b):
    b1 = b.astype(BF16)
    b2 = (b - b1.astype(F32)).astype(BF16)
    return (jnp.dot(a_bf16, b1, preferred_element_type=F32)
            + jnp.dot(a_bf16, b2, preferred_element_type=F32))


def _mm_nt(a, b):
    return lax.dot_general(a.astype(BF16), b.astype(BF16), (((1,), (1,)), ((), ())),
                           preferred_element_type=F32)


def _iota2(shape, dim):
    return lax.broadcasted_iota(jnp.int32, shape, dim)


def _hist_put(hist, tile0, cur):
    ts = cur.shape[0]
    for c in range(cur.shape[1] // LANES):
        hist[tile0 + c, HALO:HALO + ts, :] = cur[:, c * LANES:(c + 1) * LANES]


def _hist_get(hist, tile0, n, ts, delays):
    outs = [jnp.concatenate([hist[tile0 + c, HALO - d:HALO - d + ts, :] for c in range(n)], axis=1)
            for d in delays]
    for c in range(n):
        hist[tile0 + c, 0:HALO, :] = hist[tile0 + c, ts:ts + HALO, :]
    return outs


def _mixer_stream(x, w, zhist, cxhist, s_ref, result, *, ts, cw, rw):
    n_chunks = ts // WKV_CHUNK
    n_groups = rw // GROUP
    C = WKV_CHUNK

    h = _rms(x, w["g_mix"][...]).astype(BF16)

    zc = jnp.dot(h, w["w_in"][:, 0:3 * cw], preferred_element_type=F32)
    x_in = zc[:, 0:cw]
    b_gate = zc[:, cw:2 * cw]
    c_gate = zc[:, 2 * cw:3 * cw]
    cx = c_gate * x_in
    _hist_put(cxhist, 0, cx)
    yield

    zr = jnp.dot(h, w["w_in"][:, 3 * cw:], preferred_element_type=F32)
    _hist_put(zhist, 0, zr)
    yield
    cx1, cx2 = _hist_get(cxhist, 0, cw // LANES, ts, (1, 2))
    conv = cx * w["conv_w"][2:3, :] + cx1 * w["conv_w"][1:2, :] + cx2 * w["conv_w"][0:1, :]
    y_conv = b_gate * conv
    (zs,) = _hist_get(zhist, 0, zr.shape[1] // LANES, ts, (1,))
    zz = zr + (zs - zr) * w["mu"][...]
    yield
    r = zz[:, 0:rw]
    k = zz[:, rw:2 * rw]
    v = zz[:, 2 * rw:3 * rw]
    lora_in = zz[:, 3 * rw:3 * rw + 128]
    gd = zz[:, 3 * rw + 128:3 * rw + 256]

    u_dec = w["w0"][...] + _mm(jnp.tanh(lora_in), w["w_up"][...])
    logw = (-math.exp(-0.5)) * _sigmoid(u_dec)
    a = _sigmoid(w["a0"][...] + _mm(lora_in, w["a_up"][...]))
    g = _mm(_sigmoid(gd), w["g_up"][...])
    yield

    seg = (_iota2((rw, rw), 0) // HEAD == _iota2((rw, rw), 1) // HEAD).astype(BF16)

    def head_sums(t):
        return _mm(t, seg)

    kk = k * w["k_k"][...]
    kk = kk * lax.rsqrt(jnp.maximum(head_sums(kk * kk), 1e-24))
    k2 = k * (1.0 + (a - 1.0) * w["k_a"][...])
    aa = -kk
    bb = kk * a
    yield

    ri = _iota2((ts, ts), 0)
    ci = _iota2((ts, ts), 1)
    tri = jnp.where(((ri // C) == (ci // C)) & (ci <= ri), 1.0, 0.0).astype(BF16)
    cl = _mm_split2(tri, logw)
    yield
    ec = jnp.exp(cl)
    eci = 1.0 / ec
    ecm = jnp.exp(cl - logw)
    pc = jnp.concatenate(
        [jnp.broadcast_to(ec[(c + 1) * C - 1:(c + 1) * C, :], (C, rw)) for c in range(n_chunks)],
        axis=0)
    yield
    a_t = aa * ecm
    r_t = r * ec
    b_t = bb * eci
    k_t = k2 * eci
    b_h = b_t * pc
    k_h = k_t * pc
    yield

    rs = _iota2((C, GROUP), 0)
    cs = _iota2((C, GROUP), 1) % HEAD
    lower_strict = cs < rs
    lower_incl = cs <= rs
    eye_sbs = jnp.where(cs == rs, 1.0, 0.0).astype(F32)
    pair_head = (_iota2((2 * C, GROUP), 1) % (2 * HEAD)) // HEAD
    half_lane = _iota2((HEAD, LANES), 1) // HEAD

    def bd(y):
        yb = y.astype(BF16)
        zero_tile = jnp.zeros((HEAD, LANES), BF16)
        rows = []
        for hh in range(HEADS_PER_GROUP):
            lt = hh // 2
            piece = jnp.where(half_lane == hh % 2, yb[:, lt * LANES:(lt + 1) * LANES],
                              jnp.zeros((), BF16))
            rows.append(jnp.concatenate([piece, zero_tile] if lt == 0 else [zero_tile, piece],
                                        axis=1))
        return jnp.concatenate(rows, axis=0)

    state = [s_ref[gi] for gi in range(n_groups)]
    o_rows = []
    for wave_start in range(0, n_chunks, WAVE_CHUNKS):
        units = [(c, gi) for c in range(wave_start, wave_start + WAVE_CHUNKS)
                 for gi in range(n_groups)]
        nu = len(units)

        def cut(arr):
            return [arr[c * C:(c + 1) * C, gi * GROUP:(gi + 1) * GROUP] for c, gi in units]

        at, rt, bt, kt, bh, kh, vv, pcu = (cut(a_t), cut(r_t), cut(b_t), cut(k_t), cut(b_h),
                                           cut(k_h), cut(v), cut(pc))
        ar = [jnp.concatenate([at[u], rt[u]], axis=0) for u in range(nu)]
        a1 = [_mm_nt(ar[u], bd(bt[u])) for u in range(nu)]
        a2 = [_mm_nt(ar[u], bd(kt[u])) for u in range(nu)]
        a_ab = [jnp.where(lower_strict, a1[u][:C], 0.0) for u in range(nu)]
        a_rb = [jnp.where(lower_incl, a1[u][C:], 0.0) for u in range(nu)]
        a_ak = [jnp.where(lower_strict, a2[u][:C], 0.0) for u in range(nu)]
        a_rk = [jnp.where(lower_incl, a2[u][C:], 0.0) for u in range(nu)]
        yield
        av = [_mm(jnp.concatenate([a_ak[u], a_rk[u]], axis=0), bd(vv[u])) for u in range(nu)]
        yield

        lp = a_ab
        tinv = [eye_sbs + a_ab[u] for u in range(nu)]
        n_sq = int(math.log2(C))
        for i in range(n_sq):
            m = [bd(lp[u]) for u in range(nu)]
            if i == 0:
                lp = [_mm(lp[u], m[u]) for u in range(nu)]
            elif i < n_sq - 1:
                st = [_mm(jnp.concatenate([lp[u], tinv[u]], axis=0), m[u]) for u in range(nu)]
                lp = [st[u][:C] for u in range(nu)]
                tinv = [tinv[u] + st[u][C:] for u in range(nu)]
            else:
                tinv = [tinv[u] + _mm(tinv[u], m[u]) for u in range(nu)]
            yield

        tm = [jnp.concatenate([tinv[u], _mm(a_rb[u], bd(tinv[u]))], axis=0) for u in range(nu)]
        yield
        s1 = [_mm(tm[u], bd(at[u])) for u in range(nu)]
        s2 = [_mm(tm[u], bd(av[u][:C])) for u in range(nu)]
        w1 = [s1[u][:C] for u in range(nu)]
        w2 = [s2[u][:C] for u in range(nu)]
        q = [rt[u] + s1[u][C:] for u in range(nu)]
        o_loc = [s2[u][C:] + av[u][C:] for u in range(nu)]
        yield
        g_sbs, h_sbs = [], []
        for u in range(nu):
            xt = jnp.concatenate([bh[u], kh[u]], axis=0).T
            outs = []
            for pair in range(HEADS_PER_GROUP // 2):
                lhs = jnp.concatenate([xt[(2 * pair) * HEAD:(2 * pair + 1) * HEAD],
                                       xt[(2 * pair + 1) * HEAD:(2 * pair + 2) * HEAD]], axis=1)
                lanes = slice(pair * 2 * HEAD, (pair + 1) * 2 * HEAD)
                blk = jnp.concatenate(
                    [jnp.concatenate([w1[u][:, lanes], w2[u][:, lanes]], axis=1),
                     jnp.concatenate([jnp.zeros((C, 2 * HEAD), F32), vv[u][:, lanes]], axis=1)],
                    axis=0).astype(BF16)
                zero = jnp.zeros((), BF16)
                rhs = jnp.concatenate([jnp.where(pair_head == 0, blk, zero),
                                       jnp.where(pair_head == 1, blk, zero)], axis=0)
                outs.append(_mm(lhs, rhs))
            g_sbs.append(jnp.concatenate([o_[:, :2 * HEAD] for o_ in outs], axis=1)
                         + eye_sbs * pcu[u])
            h_sbs.append(jnp.concatenate([o_[:, 2 * HEAD:] for o_ in outs], axis=1))
        yield

        for cw_i in range(WAVE_CHUNKS):
            o_cols = []
            for gi in range(n_groups):
                u = cw_i * n_groups + gi
                qs = _mm(jnp.concatenate([q[u], g_sbs[u]], axis=0), bd(state[gi]))
                o_cols.append(qs[:C] + o_loc[u])
                state[gi] = qs[C:] + h_sbs[u]
            o_rows.append(jnp.concatenate(o_cols, axis=1))
            yield
    for gi in range(n_groups):
        s_ref[gi] = state[gi]
    o = jnp.concatenate(o_rows, axis=0)

    inv_n = 1.0 / HEAD
    mean = head_sums(o) * inv_n
    d = o - mean
    var = head_sums(d * d) * inv_n
    on = d * lax.rsqrt(var + GN_EPS) * w["gn_w"][...] + w["gn_b"][...]
    bonus = head_sums(r * k2 * w["r_k"][...]) * v
    y_rwkv = (on + bonus) * g
    yield

    y = jnp.concatenate([y_conv, y_rwkv], axis=1)
    result["x1"] = x + _mm(y, w["w_out"][...])


def _channel_stream(x, p, w, uhist, result, *, ts, dff, final):
    h = _rms(x, w["g_ffn"][...]).astype(BF16)
    e = _rms(_mm(p, w["w_proj"][...]), w["g_ple"][...])
    n_ff = dff // FF_CHUNK

    def up(j):
        us = []
        for half in range(2):
            lo = half * dff + j * FF_CHUNK
            us.append(jnp.dot(h, w["ffn_up"][:, lo:lo + FF_CHUNK], preferred_element_type=F32))
            _hist_put(uhist, lo // LANES, us[-1])
        return us

    def conv(u, cols):
        u1, u2 = _hist_get(uhist, cols.start // LANES, FF_CHUNK // LANES, ts, (1, 2))
        return (u * w["ffn_cw"][2:3, cols] + u1 * w["ffn_cw"][1:2, cols]
                + u2 * w["ffn_cw"][0:1, cols] + w["ffn_cb"][:, cols])

    def activation(j, u):
        gate, val = [conv(u[half], slice(half * dff + j * FF_CHUNK,
                                         half * dff + (j + 1) * FF_CHUNK)) for half in range(2)]
        return (gate * _sigmoid(gate) * val).astype(BF16)

    acc = x
    u_next = up(0)
    yield
    for j in range(n_ff):
        u = u_next
        if j + 1 < n_ff:
            u_next = up(j + 1)
            yield
        act = activation(j, u)
        yield
        acc = acc + jnp.dot(act, w["ffn_down"][j * FF_CHUNK:(j + 1) * FF_CHUNK, :],
                            preferred_element_type=F32)
        yield
    x2 = acc

    gate = _sigmoid(_mm(_rms(x2, w["g_gate"][...]), w["w_gate"][...]))
    x3 = x2 + gate * e
    result["out"] = _rms(x3, w["g_fin"][...]) if final else x3


def _run_interleaved(gen_a, n_a, gen_b, n_b, ahead):
    done_b = 0
    for ka in range(n_a):
        next(gen_a, None)
        want_b = min(n_b, -(-((ka + 1 + ahead) * n_b) // n_a))
        while done_b < want_b:
            next(gen_b, None)
            done_b += 1
    for gen in (gen_a, gen_b):
        for _ in gen:
            pass


def _layer_kernel(x_ref, p_ref, *refs, names, ts, cw, rw, dff, n_s, n_tiles, final):
    n_w = len(names)
    w = dict(zip(names, refs[:n_w]))
    o_ref = refs[n_w]
    zhist, cxhist, s_ref, uhist, x1buf = refs[n_w + 1:]

    i = pl.program_id(0)
    mix_s = lax.rem(jnp.minimum(i, n_tiles - 1), jnp.int32(n_s))
    ch_s = lax.rem(jnp.maximum(i - 1, 0), jnp.int32(n_s))

    @pl.when(i == 0)
    def _():
        x1buf[...] = jnp.zeros(x1buf.shape, F32)

    @pl.when(mix_s == 0)
    def _():
        zhist[:, 0:HALO, :] = jnp.zeros((zhist.shape[0], HALO, LANES), F32)
        cxhist[:, 0:HALO, :] = jnp.zeros((cxhist.shape[0], HALO, LANES), F32)
        s_ref[...] = jnp.zeros(s_ref.shape, F32)

    @pl.when(ch_s == 0)
    def _():
        uhist[:, 0:HALO, :] = jnp.zeros((uhist.shape[0], HALO, LANES), F32)

    slot = lax.rem(i, jnp.int32(2))
    result = {}
    mixer = _mixer_stream(x_ref[0], w, zhist, cxhist, s_ref, result, ts=ts, cw=cw, rw=rw)
    channel = _channel_stream(x1buf[1 - slot], p_ref[0], w, uhist, result,
                              ts=ts, dff=dff, final=final)
    n_waves = ts // (WKV_CHUNK * WAVE_CHUNKS)
    n_mixer = 10 + n_waves * (5 + int(math.log2(WKV_CHUNK)) + WAVE_CHUNKS)
    n_channel = 1 + 3 * (dff // FF_CHUNK)
    _run_interleaved(mixer, n_mixer, channel, n_channel, CHANNEL_AHEAD)
    x1buf[slot] = result["x1"]
    o_ref[0] = result["out"]


def _const_spec(shape):
    nd = len(shape)
    return pl.BlockSpec(shape, lambda i: (0,) * nd, pipeline_mode=pl.Buffered(1))


def _row(vec):
    return vec.reshape(1, -1).astype(F32)


def kernel(x, p, mix_norm_g, w_in, conv_mix_w, rwkv_mu, rwkv_w0, rwkv_w_up, rwkv_a0, rwkv_a_up, rwkv_g_up, rwkv_k_k, rwkv_k_a, rwkv_r_k, rwkv_gn_w, rwkv_gn_b, w_out, ffn_norm_g, ffn_w_up, ffn_conv_w, ffn_conv_b, ffn_w_down, ple_w_proj, ple_norm_g, ple_gate_norm_g, ple_w_gate, final_norm_g):
    depth = w_in.shape[0]
    bsz, seq, d = x.shape
    cw = conv_mix_w.shape[-1]
    rw = rwkv_w0.shape[-1]
    dff = ffn_w_down.shape[1]
    n_dec, n_aicl, n_gate = rwkv_w_up.shape[1], rwkv_a_up.shape[1], rwkv_g_up.shape[1]
    ts = SEQ_TILE
    assert seq % ts == 0 and ts % WKV_CHUNK == 0 and rw % GROUP == 0 and dff % FF_CHUNK == 0
    assert n_dec + n_aicl == 128 and n_gate == 128 and rwkv_r_k.shape[-1] == HEAD
    n_s = seq // ts
    n_tiles = bsz * n_s

    def mixer_tile(i):
        t = jnp.minimum(i, n_tiles - 1)
        return (t // n_s, t % n_s, 0)

    def channel_tile(i):
        t = jnp.maximum(i - 1, 0)
        return (t // n_s, t % n_s, 0)

    for li in range(depth):
        wup_pad = jnp.concatenate([rwkv_w_up[li], jnp.zeros((n_aicl, rw), F32)], axis=0)
        aup_pad = jnp.concatenate([jnp.zeros((n_dec, rw), F32), rwkv_a_up[li]], axis=0)
        weights = {
            "g_mix": _row(mix_norm_g[li]), "w_in": w_in[li].astype(BF16),
            "conv_w": conv_mix_w[li].astype(F32), "mu": _row(rwkv_mu[li]),
            "w0": _row(rwkv_w0[li]), "w_up": wup_pad.astype(BF16), "a0": _row(rwkv_a0[li]),
            "a_up": aup_pad.astype(BF16), "g_up": rwkv_g_up[li].astype(BF16),
            "k_k": _row(rwkv_k_k[li]), "k_a": _row(rwkv_k_a[li]), "r_k": _row(rwkv_r_k[li]),
            "gn_w": _row(rwkv_gn_w[li]), "gn_b": _row(rwkv_gn_b[li]),
            "w_out": w_out[li].astype(BF16),
            "g_ffn": _row(ffn_norm_g[li]), "ffn_up": ffn_w_up[li].astype(BF16),
            "ffn_cw": ffn_conv_w[li].astype(F32), "ffn_cb": _row(ffn_conv_b[li]),
            "ffn_down": ffn_w_down[li].astype(BF16), "w_proj": ple_w_proj[li].astype(BF16),
            "g_ple": _row(ple_norm_g[li]), "g_gate": _row(ple_gate_norm_g[li]),
            "w_gate": ple_w_gate[li].astype(BF16), "g_fin": _row(final_norm_g),
        }
        names = tuple(weights)
        x = pl.pallas_call(
            functools.partial(_layer_kernel, names=names, ts=ts, cw=cw, rw=rw, dff=dff,
                              n_s=n_s, n_tiles=n_tiles, final=(li == depth - 1)),
            grid=(n_tiles + 1,),
            in_specs=[pl.BlockSpec((1, ts, d), mixer_tile),
                      pl.BlockSpec((1, ts, p.shape[-1]), channel_tile)]
            + [_const_spec(weights[n].shape) for n in names],
            out_specs=pl.BlockSpec((1, ts, d), channel_tile),
            out_shape=jax.ShapeDtypeStruct(x.shape, F32),
            scratch_shapes=[
                pltpu.VMEM(((w_in.shape[-1] - 3 * cw) // LANES, HALO + ts, LANES), F32),
                pltpu.VMEM((cw // LANES, HALO + ts, LANES), F32),
                pltpu.VMEM((rw // GROUP, HEAD, GROUP), F32),
                pltpu.VMEM((2 * dff // LANES, HALO + ts, LANES), F32),
                pltpu.VMEM((2, ts, d), F32),
            ],
            compiler_params=pltpu.CompilerParams(dimension_semantics=("arbitrary",),
                                                 vmem_limit_bytes=VMEM_LIMIT_BYTES),
            name="layer",
        )(x, p[li], *[weights[n] for n in names])
    return x
```

```python
import functools
import math

import jax
import jax.numpy as jnp
from jax import lax
from jax.experimental import pallas as pl
from jax.experimental.pallas import tpu as pltpu

NORM_EPS = 1e-6
GN_EPS = 64e-5
HEAD = 64
WKV_CHUNK = 64
GROUP = 256
HEADS_PER_GROUP = GROUP // HEAD
HALO = 8
LANES = 128
SEQ_TILE = 256
FF_CHUNK = 256
VMEM_LIMIT_BYTES = 56 * 1024 * 1024
WAVE_CHUNKS = 4
CHANNEL_AHEAD = 4

F32 = jnp.float32
BF16 = jnp.bfloat16


def _rms(x, g):
    ms = jnp.mean(x * x, axis=-1, keepdims=True)
    return x * lax.rsqrt(ms + NORM_EPS) * g


def _sigmoid(x):
    return 1.0 / (1.0 + jnp.exp(-x))


def _mm(a, b):
    return jnp.dot(a.astype(BF16), b.astype(BF16), preferred_element_type=F32)


def _mm_split2(a_bf16, b):
    b1 = b.astype(BF16)
    b2 = (b - b1.astype(F32)).astype(BF16)
    return (jnp.dot(a_bf16, b1, preferred_element_type=F32)
            + jnp.dot(a_bf16, b2, preferred_element_type=F32))


def _mm_nt(a, b):
    return lax.dot_general(a.astype(BF16), b.astype(BF16), (((1,), (1,)), ((), ())),
                           preferred_element_type=F32)


def _iota2(shape, dim):
    return lax.broadcasted_iota(jnp.int32, shape, dim)


def _hist_put(hist, tile0, cur):
    ts = cur.shape[0]
    for c in range(cur.shape[1] // LANES):
        hist[tile0 + c, HALO:HALO + ts, :] = cur[:, c * LANES:(c + 1) * LANES]


def _hist_get(hist, tile0, n, ts, delays):
    outs = [jnp.concatenate([hist[tile0 + c, HALO - d:HALO - d + ts, :] for c in range(n)], axis=1)
            for d in delays]
    for c in range(n):
        hist[tile0 + c, 0:HALO, :] = hist[tile0 + c, ts:ts + HALO, :]
    return outs


def _mixer_stream(x, w, zhist, cxhist, s_ref, result, *, ts, cw, rw):
    n_chunks = ts // WKV_CHUNK
    n_groups = rw // GROUP
    C = WKV_CHUNK

    h = _rms(x, w["g_mix"][...]).astype(BF16)

    zc = jnp.dot(h, w["w_in"][:, 0:3 * cw], preferred_element_type=F32)
    x_in = zc[:, 0:cw]
    b_gate = zc[:, cw:2 * cw]
    c_gate = zc[:, 2 * cw:3 * cw]
    cx = c_gate * x_in
    _hist_put(cxhist, 0, cx)
    yield

    zr = jnp.dot(h, w["w_in"][:, 3 * cw:], preferred_element_type=F32)
    _hist_put(zhist, 0, zr)
    yield
    cx1, cx2 = _hist_get(cxhist, 0, cw // LANES, ts, (1, 2))
    conv = cx * w["conv_w"][2:3, :] + cx1 * w["conv_w"][1:2, :] + cx2 * w["conv_w"][0:1, :]
    y_conv = b_gate * conv
    (zs,) = _hist_get(zhist, 0, zr.shape[1] // LANES, ts, (1,))
    zz = zr + (zs - zr) * w["mu"][...]
    yield
    r = zz[:, 0:rw]
    k = zz[:, rw:2 * rw]
    v = zz[:, 2 * rw:3 * rw]
    lora_in = zz[:, 3 * rw:3 * rw + 128]
    gd = zz[:, 3 * rw + 128:3 * rw + 256]

    u_dec = w["w0"][...] + _mm(jnp.tanh(lora_in), w["w_up"][...])
    logw = (-math.exp(-0.5)) * _sigmoid(u_dec)
    a = _sigmoid(w["a0"][...] + _mm(lora_in, w["a_up"][...]))
    g = _mm(_sigmoid(gd), w["g_up"][...])
    yield

    seg = (_iota2((rw, rw), 0) // HEAD == _iota2((rw, rw), 1) // HEAD).astype(BF16)

    def head_sums(t):
        return _mm(t, seg)

    kk = k * w["k_k"][...]
    kk = kk * lax.rsqrt(jnp.maximum(head_sums(kk * kk), 1e-24))
    k2 = k * (1.0 + (a - 1.0) * w["k_a"][...])
    aa = -kk
    bb = kk * a
    yield

    ri = _iota2((ts, ts), 0)
    ci = _iota2((ts, ts), 1)
    tri = jnp.where(((ri // C) == (ci // C)) & (ci <= ri), 1.0, 0.0).astype(BF16)
    cl = _mm_split2(tri, logw)
    yield
    ec = jnp.exp(cl)
    eci = 1.0 / ec
    ecm = jnp.exp(cl - logw)
    pc = jnp.concatenate(
        [jnp.broadcast_to(ec[(c + 1) * C - 1:(c + 1) * C, :], (C, rw)) for c in range(n_chunks)],
        axis=0)
    yield
    a_t = aa * ecm
    r_t = r * ec
    b_t = bb * eci
    k_t = k2 * eci
    b_h = b_t * pc
    k_h = k_t * pc
    yield

    rs = _iota2((C, GROUP), 0)
    cs = _iota2((C, GROUP), 1) % HEAD
    lower_strict = cs < rs
    lower_incl = cs <= rs
    eye_sbs = jnp.where(cs == rs, 1.0, 0.0).astype(F32)
    pair_head = (_iota2((2 * C, GROUP), 1) % (2 * HEAD)) // HEAD
    half_lane = _iota2((HEAD, LANES), 1) // HEAD

    def bd(y):
        yb = y.astype(BF16)
        zero_tile = jnp.zeros((HEAD, LANES), BF16)
        rows = []
        for hh in range(HEADS_PER_GROUP):
            lt = hh // 2
            piece = jnp.where(half_lane == hh % 2, yb[:, lt * LANES:(lt + 1) * LANES],
                              jnp.zeros((), BF16))
            rows.append(jnp.concatenate([piece, zero_tile] if lt == 0 else [zero_tile, piece],
                                        axis=1))
        return jnp.concatenate(rows, axis=0)

    state = [s_ref[gi] for gi in range(n_groups)]
    o_rows = []
    for wave_start in range(0, n_chunks, WAVE_CHUNKS):
        units = [(c, gi) for c in range(wave_start, wave_start + WAVE_CHUNKS)
                 for gi in range(n_groups)]
        nu = len(units)

        def cut(arr):
            return [arr[c * C:(c + 1) * C, gi * GROUP:(gi + 1) * GROUP] for c, gi in units]

        at, rt, bt, kt, bh, kh, vv, pcu = (cut(a_t), cut(r_t), cut(b_t), cut(k_t), cut(b_h),
                                           cut(k_h), cut(v), cut(pc))
        ar = [jnp.concatenate([at[u], rt[u]], axis=0) for u in range(nu)]
        a1 = [_mm_nt(ar[u], bd(bt[u])) for u in range(nu)]
        a2 = [_mm_nt(ar[u], bd(kt[u])) for u in range(nu)]
        a_ab = [jnp.where(lower_strict, a1[u][:C], 0.0) for u in range(nu)]
        a_rb = [jnp.where(lower_incl, a1[u][C:], 0.0) for u in range(nu)]
        a_ak = [jnp.where(lower_strict, a2[u][:C], 0.0) for u in range(nu)]
        a_rk = [jnp.where(lower_incl, a2[u][C:], 0.0) for u in range(nu)]
        yield
        av = [_mm(jnp.concatenate([a_ak[u], a_rk[u]], axis=0), bd(vv[u])) for u in range(nu)]
        yield

        lp = a_ab
        tinv = [eye_sbs + a_ab[u] for u in range(nu)]
        n_sq = int(math.log2(C))
        for i in range(n_sq):
            m = [bd(lp[u]) for u in range(nu)]
            if i == 0:
                lp = [_mm(lp[u], m[u]) for u in range(nu)]
            elif i < n_sq - 1:
                st = [_mm(jnp.concatenate([lp[u], tinv[u]], axis=0), m[u]) for u in range(nu)]
                lp = [st[u][:C] for u in range(nu)]
                tinv = [tinv[u] + st[u][C:] for u in range(nu)]
            else:
                tinv = [tinv[u] + _mm(tinv[u], m[u]) for u in range(nu)]
            yield

        tm = [jnp.concatenate([tinv[u], _mm(a_rb[u], bd(tinv[u]))], axis=0) for u in range(nu)]
        yield
        s1 = [_mm(tm[u], bd(at[u])) for u in range(nu)]
        s2 = [_mm(tm[u], bd(av[u][:C])) for u in range(nu)]
        w1 = [s1[u][:C] for u in range(nu)]
        w2 = [s2[u][:C] for u in range(nu)]
        q = [rt[u] + s1[u][C:] for u in range(nu)]
        o_loc = [s2[u][C:] + av[u][C:] for u in range(nu)]
        yield
        g_sbs, h_sbs = [], []
        for u in range(nu):
            xt = jnp.concatenate([bh[u], kh[u]], axis=0).T
            outs = []
            for pair in range(HEADS_PER_GROUP // 2):
                lhs = jnp.concatenate([xt[(2 * pair) * HEAD:(2 * pair + 1) * HEAD],
                                       xt[(2 * pair + 1) * HEAD:(2 * pair + 2) * HEAD]], axis=1)
                lanes = slice(pair * 2 * HEAD, (pair + 1) * 2 * HEAD)
                blk = jnp.concatenate(
                    [jnp.concatenate([w1[u][:, lanes], w2[u][:, lanes]], axis=1),
                     jnp.concatenate([jnp.zeros((C, 2 * HEAD), F32), vv[u][:, lanes]], axis=1)],
                    axis=0).astype(BF16)
                zero = jnp.zeros((), BF16)
                rhs = jnp.concatenate([jnp.where(pair_head == 0, blk, zero),
                                       jnp.where(pair_head == 1, blk, zero)], axis=0)
                outs.append(_mm(lhs, rhs))
            g_sbs.append(jnp.concatenate([o_[:, :2 * HEAD] for o_ in outs], axis=1)
                         + eye_sbs * pcu[u])
            h_sbs.append(jnp.concatenate([o_[:, 2 * HEAD:] for o_ in outs], axis=1))
        yield

        for cw_i in range(WAVE_CHUNKS):
            o_cols = []
            for gi in range(n_groups):
                u = cw_i * n_groups + gi
                qs = _mm(jnp.concatenate([q[u], g_sbs[u]], axis=0), bd(state[gi]))
                o_cols.append(qs[:C] + o_loc[u])
                state[gi] = qs[C:] + h_sbs[u]
            o_rows.append(jnp.concatenate(o_cols, axis=1))
            yield
    for gi in range(n_groups):
        s_ref[gi] = state[gi]
    o = jnp.concatenate(o_rows, axis=0)

    inv_n = 1.0 / HEAD
    mean = head_sums(o) * inv_n
    d = o - mean
    var = head_sums(d * d) * inv_n
    on = d * lax.rsqrt(var + GN_EPS) * w["gn_w"][...] + w["gn_b"][...]
    bonus = head_sums(r * k2 * w["r_k"][...]) * v
    y_rwkv = (on + bonus) * g
    yield

    y = jnp.concatenate([y_conv, y_rwkv], axis=1)
    result["x1"] = x + _mm(y, w["w_out"][...])


def _channel_stream(x, p, w, uhist, result, *, ts, dff, final):
    h = _rms(x, w["g_ffn"][...]).astype(BF16)
    e = _rms(_mm(p, w["w_proj"][...]), w["g_ple"][...])
    n_ff = dff // FF_CHUNK

    def up(j):
        cols = slice(2 * j * FF_CHUNK, 2 * (j + 1) * FF_CHUNK)
        u = jnp.dot(h, w["ffn_up"][:, cols], preferred_element_type=F32)
        _hist_put(uhist, cols.start // LANES, u)
        return u

    def activation(j, u):
        cols = slice(2 * j * FF_CHUNK, 2 * (j + 1) * FF_CHUNK)
        u1, u2 = _hist_get(uhist, cols.start // LANES, 2 * FF_CHUNK // LANES, ts, (1, 2))
        uc = (u * w["ffn_cw"][2:3, cols] + u1 * w["ffn_cw"][1:2, cols]
              + u2 * w["ffn_cw"][0:1, cols] + w["ffn_cb"][:, cols])
        gate, val = uc[:, :FF_CHUNK], uc[:, FF_CHUNK:]
        return (gate * _sigmoid(gate) * val).astype(BF16)

    acc = x
    u_next = up(0)
    yield
    for j in range(n_ff):
        u = u_next
        if j + 1 < n_ff:
            u_next = up(j + 1)
            yield
        act = activation(j, u)
        yield
        acc = acc + jnp.dot(act, w["ffn_down"][j * FF_CHUNK:(j + 1) * FF_CHUNK, :],
                            preferred_element_type=F32)
        yield
    x2 = acc

    gate = _sigmoid(_mm(_rms(x2, w["g_gate"][...]), w["w_gate"][...]))
    x3 = x2 + gate * e
    result["out"] = _rms(x3, w["g_fin"][...]) if final else x3


def _run_interleaved(gen_a, n_a, gen_b, n_b, ahead):
    done_b = 0
    for ka in range(n_a):
        next(gen_a, None)
        want_b = min(n_b, -(-((ka + 1 + ahead) * n_b) // n_a))
        while done_b < want_b:
            next(gen_b, None)
            done_b += 1
    for gen in (gen_a, gen_b):
        for _ in gen:
            pass


def _layer_kernel(x_ref, p_ref, *refs, names, ts, cw, rw, dff, n_s, n_tiles, final):
    n_w = len(names)
    w = dict(zip(names, refs[:n_w]))
    o_ref = refs[n_w]
    zhist, cxhist, s_ref, uhist, x1buf = refs[n_w + 1:]

    i = pl.program_id(0)
    mix_s = lax.rem(jnp.minimum(i, n_tiles - 1), jnp.int32(n_s))
    ch_s = lax.rem(jnp.maximum(i - 1, 0), jnp.int32(n_s))

    @pl.when(i == 0)
    def _():
        x1buf[...] = jnp.zeros(x1buf.shape, F32)

    @pl.when(mix_s == 0)
    def _():
        zhist[:, 0:HALO, :] = jnp.zeros((zhist.shape[0], HALO, LANES), F32)
        cxhist[:, 0:HALO, :] = jnp.zeros((cxhist.shape[0], HALO, LANES), F32)
        s_ref[...] = jnp.zeros(s_ref.shape, F32)

    @pl.when(ch_s == 0)
    def _():
        uhist[:, 0:HALO, :] = jnp.zeros((uhist.shape[0], HALO, LANES), F32)

    slot = lax.rem(i, jnp.int32(2))
    result = {}
    mixer = _mixer_stream(x_ref[0], w, zhist, cxhist, s_ref, result, ts=ts, cw=cw, rw=rw)
    channel = _channel_stream(x1buf[1 - slot], p_ref[0], w, uhist, result,
                              ts=ts, dff=dff, final=final)
    n_waves = ts // (WKV_CHUNK * WAVE_CHUNKS)
    n_mixer = 10 + n_waves * (5 + int(math.log2(WKV_CHUNK)) + WAVE_CHUNKS)
    n_channel = 1 + 3 * (dff // FF_CHUNK)
    _run_interleaved(mixer, n_mixer, channel, n_channel, CHANNEL_AHEAD)
    x1buf[slot] = result["x1"]
    o_ref[0] = result["out"]


def _const_spec(shape):
    nd = len(shape)
    return pl.BlockSpec(shape, lambda i: (0,) * nd, pipeline_mode=pl.Buffered(1))


def _pair_chunks(wm, dff):
    parts = []
    for j in range(dff // FF_CHUNK):
        parts.append(wm[:, j * FF_CHUNK:(j + 1) * FF_CHUNK])
        parts.append(wm[:, dff + j * FF_CHUNK:dff + (j + 1) * FF_CHUNK])
    return jnp.concatenate(parts, axis=1)


def _row(vec):
    return vec.reshape(1, -1).astype(F32)


def kernel(x, p, mix_norm_g, w_in, conv_mix_w, rwkv_mu, rwkv_w0, rwkv_w_up, rwkv_a0, rwkv_a_up, rwkv_g_up, rwkv_k_k, rwkv_k_a, rwkv_r_k, rwkv_gn_w, rwkv_gn_b, w_out, ffn_norm_g, ffn_w_up, ffn_conv_w, ffn_conv_b, ffn_w_down, ple_w_proj, ple_norm_g, ple_gate_norm_g, ple_w_gate, final_norm_g):
    depth = w_in.shape[0]
    bsz, seq, d = x.shape
    cw = conv_mix_w.shape[-1]
    rw = rwkv_w0.shape[-1]
    dff = ffn_w_down.shape[1]
    n_dec, n_aicl, n_gate = rwkv_w_up.shape[1], rwkv_a_up.shape[1], rwkv_g_up.shape[1]
    ts = SEQ_TILE
    assert seq % ts == 0 and ts % WKV_CHUNK == 0 and rw % GROUP == 0 and dff % FF_CHUNK == 0
    assert n_dec + n_aicl == 128 and n_gate == 128 and rwkv_r_k.shape[-1] == HEAD
    n_s = seq // ts
    n_tiles = bsz * n_s

    def mixer_tile(i):
        t = jnp.minimum(i, n_tiles - 1)
        return (t // n_s, t % n_s, 0)

    def channel_tile(i):
        t = jnp.maximum(i - 1, 0)
        return (t // n_s, t % n_s, 0)

    for li in range(depth):
        wup_pad = jnp.concatenate([rwkv_w_up[li], jnp.zeros((n_aicl, rw), F32)], axis=0)
        aup_pad = jnp.concatenate([jnp.zeros((n_dec, rw), F32), rwkv_a_up[li]], axis=0)
        weights = {
            "g_mix": _row(mix_norm_g[li]), "w_in": w_in[li].astype(BF16),
            "conv_w": conv_mix_w[li].astype(F32), "mu": _row(rwkv_mu[li]),
            "w0": _row(rwkv_w0[li]), "w_up": wup_pad.astype(BF16), "a0": _row(rwkv_a0[li]),
            "a_up": aup_pad.astype(BF16), "g_up": rwkv_g_up[li].astype(BF16),
            "k_k": _row(rwkv_k_k[li]), "k_a": _row(rwkv_k_a[li]), "r_k": _row(rwkv_r_k[li]),
            "gn_w": _row(rwkv_gn_w[li]), "gn_b": _row(rwkv_gn_b[li]),
            "w_out": w_out[li].astype(BF16),
            "g_ffn": _row(ffn_norm_g[li]),
            "ffn_up": _pair_chunks(ffn_w_up[li], dff).astype(BF16),
            "ffn_cw": _pair_chunks(ffn_conv_w[li].astype(F32), dff),
            "ffn_cb": _pair_chunks(_row(ffn_conv_b[li]), dff),
            "ffn_down": ffn_w_down[li].astype(BF16), "w_proj": ple_w_proj[li].astype(BF16),
            "g_ple": _row(ple_norm_g[li]), "g_gate": _row(ple_gate_norm_g[li]),
            "w_gate": ple_w_gate[li].astype(BF16), "g_fin": _row(final_norm_g),
        }
        names = tuple(weights)
        x = pl.pallas_call(
            functools.partial(_layer_kernel, names=names, ts=ts, cw=cw, rw=rw, dff=dff,
                              n_s=n_s, n_tiles=n_tiles, final=(li == depth - 1)),
            grid=(n_tiles + 1,),
            in_specs=[pl.BlockSpec((1, ts, d), mixer_tile),
                      pl.BlockSpec((1, ts, p.shape[-1]), channel_tile)]
            + [_const_spec(weights[n].shape) for n in names],
            out_specs=pl.BlockSpec((1, ts, d), channel_tile),
            out_shape=jax.ShapeDtypeStruct(x.shape, F32),
            scratch_shapes=[
                pltpu.VMEM(((w_in.shape[-1] - 3 * cw) // LANES, HALO + ts, LANES), F32),
                pltpu.VMEM((cw // LANES, HALO + ts, LANES), F32),
                pltpu.VMEM((rw // GROUP, HEAD, GROUP), F32),
                pltpu.VMEM((2 * dff // LANES, HALO + ts, LANES), F32),
                pltpu.VMEM((2, ts, d), F32),
            ],
            compiler_params=pltpu.CompilerParams(dimension_semantics=("arbitrary",),
                                                 vmem_limit_bytes=VMEM_LIMIT_BYTES),
            name="layer",
        )(x, p[li], *[weights[n] for n in names])
    return x
```

```python
import functools
import math

import jax
import jax.numpy as jnp
from jax import lax
from jax.experimental import pallas as pl
from jax.experimental.pallas import tpu as pltpu

NORM_EPS = 1e-6
GN_EPS = 64e-5
HEAD = 64
WKV_CHUNK = 64
GROUP = 256
HEADS_PER_GROUP = GROUP // HEAD
HALO = 8
LANES = 128
SEQ_TILE = 256
FF_CHUNK = 256
VMEM_LIMIT_BYTES = 56 * 1024 * 1024
WAVE_CHUNKS = 4
CHANNEL_AHEAD = 4

F32 = jnp.float32
BF16 = jnp.bfloat16


def _rms(x, g):
    ms = jnp.mean(x * x, axis=-1, keepdims=True)
    return x * lax.rsqrt(ms + NORM_EPS) * g


def _sigmoid(x):
    return 1.0 / (1.0 + jnp.exp(-x))


def _mm(a, b):
    return jnp.dot(a.astype(BF16), b.astype(BF16), preferred_element_type=F32)


def _mm_split2(a_bf16, b):
    b1 = b.astype(BF16)
    b2 = (b - b1.astype(F32)).astype(BF16)
    return (jnp.dot(a_bf16, b1, preferred_element_type=F32)
            + jnp.dot(a_bf16, b2, preferred_element_type=F32))


def _iota2(shape, dim):
    return lax.broadcasted_iota(jnp.int32, shape, dim)


def _hist_put(hist, tile0, cur):
    ts = cur.shape[0]
    for c in range(cur.shape[1] // LANES):
        hist[tile0 + c, HALO:HALO + ts, :] = cur[:, c * LANES:(c + 1) * LANES]


def _hist_get(hist, tile0, n, ts, delays):
    outs = [jnp.concatenate([hist[tile0 + c, HALO - d:HALO - d + ts, :] for c in range(n)], axis=1)
            for d in delays]
    for c in range(n):
        hist[tile0 + c, 0:HALO, :] = hist[tile0 + c, ts:ts + HALO, :]
    return outs


def _mixer_stream(x, w, zhist, cxhist, s_ref, result, *, ts, cw, rw):
    n_chunks = ts // WKV_CHUNK
    n_groups = rw // GROUP
    C = WKV_CHUNK

    h = _rms(x, w["g_mix"][...]).astype(BF16)

    zc = jnp.dot(h, w["w_in"][:, 0:3 * cw], preferred_element_type=F32)
    x_in = zc[:, 0:cw]
    b_gate = zc[:, cw:2 * cw]
    c_gate = zc[:, 2 * cw:3 * cw]
    cx = c_gate * x_in
    _hist_put(cxhist, 0, cx)
    yield

    zr = jnp.dot(h, w["w_in"][:, 3 * cw:], preferred_element_type=F32)
    _hist_put(zhist, 0, zr)
    yield
    cx1, cx2 = _hist_get(cxhist, 0, cw // LANES, ts, (1, 2))
    conv = cx * w["conv_w"][2:3, :] + cx1 * w["conv_w"][1:2, :] + cx2 * w["conv_w"][0:1, :]
    y_conv = b_gate * conv
    (zs,) = _hist_get(zhist, 0, zr.shape[1] // LANES, ts, (1,))
    zz = zr + (zs - zr) * w["mu"][...]
    yield
    r = zz[:, 0:rw]
    k = zz[:, rw:2 * rw]
    v = zz[:, 2 * rw:3 * rw]
    lora_in = zz[:, 3 * rw:3 * rw + 128]
    gd = zz[:, 3 * rw + 128:3 * rw + 256]

    u_dec = w["w0"][...] + _mm(jnp.tanh(lora_in), w["w_up"][...])
    logw = (-math.exp(-0.5)) * _sigmoid(u_dec)
    a = _sigmoid(w["a0"][...] + _mm(lora_in, w["a_up"][...]))
    g = _mm(_sigmoid(gd), w["g_up"][...])
    yield

    seg = (_iota2((rw, rw), 0) // HEAD == _iota2((rw, rw), 1) // HEAD).astype(BF16)

    def head_sums(t):
        return _mm(t, seg)

    kk = k * w["k_k"][...]
    kk = kk * lax.rsqrt(jnp.maximum(head_sums(kk * kk), 1e-24))
    k2 = k * (1.0 + (a - 1.0) * w["k_a"][...])
    aa = -kk
    bb = kk * a
    yield

    ri = _iota2((ts, ts), 0)
    ci = _iota2((ts, ts), 1)
    tri = jnp.where(((ri // C) == (ci // C)) & (ci <= ri), 1.0, 0.0).astype(BF16)
    cl = _mm_split2(tri, logw)
    yield
    ec = jnp.exp(cl)
    eci = 1.0 / ec
    ecm = jnp.exp(cl - logw)
    pc = jnp.concatenate(
        [jnp.broadcast_to(ec[(c + 1) * C - 1:(c + 1) * C, :], (C, rw)) for c in range(n_chunks)],
        axis=0)
    yield
    a_t = aa * ecm
    r_t = r * ec
    b_t = bb * eci
    k_t = k2 * eci
    b_h = b_t * pc
    k_h = k_t * pc
    yield

    rs = _iota2((C, GROUP), 0)
    cs = _iota2((C, GROUP), 1) % HEAD
    lower_strict = cs < rs
    lower_incl = cs <= rs
    eye_sbs = jnp.where(cs == rs, 1.0, 0.0).astype(F32)
    pair_head = (_iota2((2 * C, GROUP), 1) % (2 * HEAD)) // HEAD
    half_lane = _iota2((HEAD, LANES), 1) // HEAD

    def bd(y):
        yb = y.astype(BF16)
        zero_tile = jnp.zeros((HEAD, LANES), BF16)
        rows = []
        for hh in range(HEADS_PER_GROUP):
            lt = hh // 2
            piece = jnp.where(half_lane == hh % 2, yb[:, lt * LANES:(lt + 1) * LANES],
                              jnp.zeros((), BF16))
            rows.append(jnp.concatenate([piece, zero_tile] if lt == 0 else [zero_tile, piece],
                                        axis=1))
        return jnp.concatenate(rows, axis=0)

    b_tT = b_t.T
    k_tT = k_t.T

    def bd_t(xt, c, gi):
        lt, hc = c // 2, c % 2
        zero_tile = jnp.zeros((HEAD, LANES), BF16)
        rows = []
        for hh in range(HEADS_PER_GROUP):
            src = xt[gi * GROUP + hh * HEAD:gi * GROUP + (hh + 1) * HEAD,
                     lt * LANES:(lt + 1) * LANES]
            if hc != hh % 2:
                src = pltpu.roll(src, HEAD, axis=1)
            piece = jnp.where(half_lane == hh % 2, src.astype(BF16), jnp.zeros((), BF16))
            rows.append(jnp.concatenate([piece, zero_tile] if hh // 2 == 0 else [zero_tile, piece],
                                        axis=1))
        return jnp.concatenate(rows, axis=0)

    state = [s_ref[gi] for gi in range(n_groups)]
    o_rows = []
    for wave_start in range(0, n_chunks, WAVE_CHUNKS):
        units = [(c, gi) for c in range(wave_start, wave_start + WAVE_CHUNKS)
                 for gi in range(n_groups)]
        nu = len(units)

        def cut(arr):
            return [arr[c * C:(c + 1) * C, gi * GROUP:(gi + 1) * GROUP] for c, gi in units]

        at, rt, bh, kh, vv, pcu = (cut(a_t), cut(r_t), cut(b_h), cut(k_h), cut(v), cut(pc))
        ar = [jnp.concatenate([at[u], rt[u]], axis=0) for u in range(nu)]
        a1 = [_mm(ar[u], bd_t(b_tT, *units[u])) for u in range(nu)]
        a2 = [_mm(ar[u], bd_t(k_tT, *units[u])) for u in range(nu)]
        a_ab = [jnp.where(lower_strict, a1[u][:C], 0.0) for u in range(nu)]
        a_rb = [jnp.where(lower_incl, a1[u][C:], 0.0) for u in range(nu)]
        a_ak = [jnp.where(lower_strict, a2[u][:C], 0.0) for u in range(nu)]
        a_rk = [jnp.where(lower_incl, a2[u][C:], 0.0) for u in range(nu)]
        yield
        av = [_mm(jnp.concatenate([a_ak[u], a_rk[u]], axis=0), bd(vv[u])) for u in range(nu)]
        yield

        lp = a_ab
        tinv = [eye_sbs + a_ab[u] for u in range(nu)]
        n_sq = int(math.log2(C))
        for i in range(n_sq):
            m = [bd(lp[u]) for u in range(nu)]
            if i == 0:
                lp = [_mm(lp[u], m[u]) for u in range(nu)]
            elif i < n_sq - 1:
                st = [_mm(jnp.concatenate([lp[u], tinv[u]], axis=0), m[u]) for u in range(nu)]
                lp = [st[u][:C] for u in range(nu)]
                tinv = [tinv[u] + st[u][C:] for u in range(nu)]
            else:
                tinv = [tinv[u] + _mm(tinv[u], m[u]) for u in range(nu)]
            yield

        tm = [jnp.concatenate([tinv[u], _mm(a_rb[u], bd(tinv[u]))], axis=0) for u in range(nu)]
        yield
        s1 = [_mm(tm[u], bd(at[u])) for u in range(nu)]
        s2 = [_mm(tm[u], bd(av[u][:C])) for u in range(nu)]
        w1 = [s1[u][:C] for u in range(nu)]
        w2 = [s2[u][:C] for u in range(nu)]
        q = [rt[u] + s1[u][C:] for u in range(nu)]
        o_loc = [s2[u][C:] + av[u][C:] for u in range(nu)]
        yield
        g_sbs, h_sbs = [], []
        for u in range(nu):
            xt = jnp.concatenate([bh[u], kh[u]], axis=0).T
            outs = []
            for pair in range(HEADS_PER_GROUP // 2):
                lhs = jnp.concatenate([xt[(2 * pair) * HEAD:(2 * pair + 1) * HEAD],
                                       xt[(2 * pair + 1) * HEAD:(2 * pair + 2) * HEAD]], axis=1)
                lanes = slice(pair * 2 * HEAD, (pair + 1) * 2 * HEAD)
                blk = jnp.concatenate(
                    [jnp.concatenate([w1[u][:, lanes], w2[u][:, lanes]], axis=1),
                     jnp.concatenate([jnp.zeros((C, 2 * HEAD), F32), vv[u][:, lanes]], axis=1)],
                    axis=0).astype(BF16)
                zero = jnp.zeros((), BF16)
                rhs = jnp.concatenate([jnp.where(pair_head == 0, blk, zero),
                                       jnp.where(pair_head == 1, blk, zero)], axis=0)
                outs.append(_mm(lhs, rhs))
            g_sbs.append(jnp.concatenate([o_[:, :2 * HEAD] for o_ in outs], axis=1)
                         + eye_sbs * pcu[u])
            h_sbs.append(jnp.concatenate([o_[:, 2 * HEAD:] for o_ in outs], axis=1))
        yield

        for cw_i in range(WAVE_CHUNKS):
            o_cols = []
            for gi in range(n_groups):
                u = cw_i * n_groups + gi
                qs = _mm(jnp.concatenate([q[u], g_sbs[u]], axis=0), bd(state[gi]))
                o_cols.append(qs[:C] + o_loc[u])
                state[gi] = qs[C:] + h_sbs[u]
            o_rows.append(jnp.concatenate(o_cols, axis=1))
            yield
    for gi in range(n_groups):
        s_ref[gi] = state[gi]
    o = jnp.concatenate(o_rows, axis=0)

    inv_n = 1.0 / HEAD
    mean = head_sums(o) * inv_n
    d = o - mean
    var = head_sums(d * d) * inv_n
    on = d * lax.rsqrt(var + GN_EPS) * w["gn_w"][...] + w["gn_b"][...]
    bonus = head_sums(r * k2 * w["r_k"][...]) * v
    y_rwkv = (on + bonus) * g
    yield

    y = jnp.concatenate([y_conv, y_rwkv], axis=1)
    result["x1"] = x + _mm(y, w["w_out"][...])


def _channel_stream(x, p, w, uhist, result, *, ts, dff, final):
    h = _rms(x, w["g_ffn"][...]).astype(BF16)
    e = _rms(_mm(p, w["w_proj"][...]), w["g_ple"][...])
    n_ff = dff // FF_CHUNK

    def up(j):
        us = []
        for half in range(2):
            lo = half * dff + j * FF_CHUNK
            us.append(jnp.dot(h, w["ffn_up"][:, lo:lo + FF_CHUNK], preferred_element_type=F32))
            _hist_put(uhist, lo // LANES, us[-1])
        return us

    def conv(u, cols):
        u1, u2 = _hist_get(uhist, cols.start // LANES, FF_CHUNK // LANES, ts, (1, 2))
        return (u * w["ffn_cw"][2:3, cols] + u1 * w["ffn_cw"][1:2, cols]
                + u2 * w["ffn_cw"][0:1, cols] + w["ffn_cb"][:, cols])

    def activation(j, u):
        gate, val = [conv(u[half], slice(half * dff + j * FF_CHUNK,
                                         half * dff + (j + 1) * FF_CHUNK)) for half in range(2)]
        return (gate * _sigmoid(gate) * val).astype(BF16)

    acc = x
    u_next = up(0)
    yield
    for j in range(n_ff):
        u = u_next
        if j + 1 < n_ff:
            u_next = up(j + 1)
            yield
        act = activation(j, u)
        yield
        acc = acc + jnp.dot(act, w["ffn_down"][j * FF_CHUNK:(j + 1) * FF_CHUNK, :],
                            preferred_element_type=F32)
        yield
    x2 = acc

    gate = _sigmoid(_mm(_rms(x2, w["g_gate"][...]), w["w_gate"][...]))
    x3 = x2 + gate * e
    result["out"] = _rms(x3, w["g_fin"][...]) if final else x3


def _run_interleaved(gen_a, n_a, gen_b, n_b, ahead):
    done_b = 0
    for ka in range(n_a):
        next(gen_a, None)
        want_b = min(n_b, -(-((ka + 1 + ahead) * n_b) // n_a))
        while done_b < want_b:
            next(gen_b, None)
            done_b += 1
    for gen in (gen_a, gen_b):
        for _ in gen:
            pass


def _layer_kernel(x_ref, p_ref, *refs, names, ts, cw, rw, dff, n_s, n_tiles, final):
    n_w = len(names)
    w = dict(zip(names, refs[:n_w]))
    o_ref = refs[n_w]
    zhist, cxhist, s_ref, uhist, x1buf = refs[n_w + 1:]

    i = pl.program_id(0)
    mix_s = lax.rem(jnp.minimum(i, n_tiles - 1), jnp.int32(n_s))
    ch_s = lax.rem(jnp.maximum(i - 1, 0), jnp.int32(n_s))

    @pl.when(i == 0)
    def _():
        x1buf[...] = jnp.zeros(x1buf.shape, F32)

    @pl.when(mix_s == 0)
    def _():
        zhist[:, 0:HALO, :] = jnp.zeros((zhist.shape[0], HALO, LANES), F32)
        cxhist[:, 0:HALO, :] = jnp.zeros((cxhist.shape[0], HALO, LANES), F32)
        s_ref[...] = jnp.zeros(s_ref.shape, F32)

    @pl.when(ch_s == 0)
    def _():
        uhist[:, 0:HALO, :] = jnp.zeros((uhist.shape[0], HALO, LANES), F32)

    slot = lax.rem(i, jnp.int32(2))
    result = {}
    mixer = _mixer_stream(x_ref[0], w, zhist, cxhist, s_ref, result, ts=ts, cw=cw, rw=rw)
    channel = _channel_stream(x1buf[1 - slot], p_ref[0], w, uhist, result,
                              ts=ts, dff=dff, final=final)
    n_waves = ts // (WKV_CHUNK * WAVE_CHUNKS)
    n_mixer = 10 + n_waves * (5 + int(math.log2(WKV_CHUNK)) + WAVE_CHUNKS)
    n_channel = 1 + 3 * (dff // FF_CHUNK)
    _run_interleaved(mixer, n_mixer, channel, n_channel, CHANNEL_AHEAD)
    x1buf[slot] = result["x1"]
    o_ref[0] = result["out"]


def _const_spec(shape):
    nd = len(shape)
    return pl.BlockSpec(shape, lambda i: (0,) * nd, pipeline_mode=pl.Buffered(1))


def _row(vec):
    return vec.reshape(1, -1).astype(F32)


def kernel(x, p, mix_norm_g, w_in, conv_mix_w, rwkv_mu, rwkv_w0, rwkv_w_up, rwkv_a0, rwkv_a_up, rwkv_g_up, rwkv_k_k, rwkv_k_a, rwkv_r_k, rwkv_gn_w, rwkv_gn_b, w_out, ffn_norm_g, ffn_w_up, ffn_conv_w, ffn_conv_b, ffn_w_down, ple_w_proj, ple_norm_g, ple_gate_norm_g, ple_w_gate, final_norm_g):
    depth = w_in.shape[0]
    bsz, seq, d = x.shape
    cw = conv_mix_w.shape[-1]
    rw = rwkv_w0.shape[-1]
    dff = ffn_w_down.shape[1]
    n_dec, n_aicl, n_gate = rwkv_w_up.shape[1], rwkv_a_up.shape[1], rwkv_g_up.shape[1]
    ts = SEQ_TILE
    assert seq % ts == 0 and ts % (WKV_CHUNK * WAVE_CHUNKS) == 0
    assert rw % GROUP == 0 and dff % FF_CHUNK == 0
    assert n_dec + n_aicl == 128 and n_gate == 128 and rwkv_r_k.shape[-1] == HEAD
    n_s = seq // ts
    n_tiles = bsz * n_s

    def mixer_tile(i):
        t = jnp.minimum(i, n_tiles - 1)
        return (t // n_s, t % n_s, 0)

    def channel_tile(i):
        t = jnp.maximum(i - 1, 0)
        return (t // n_s, t % n_s, 0)

    for li in range(depth):
        wup_pad = jnp.concatenate([rwkv_w_up[li], jnp.zeros((n_aicl, rw), F32)], axis=0)
        aup_pad = jnp.concatenate([jnp.zeros((n_dec, rw), F32), rwkv_a_up[li]], axis=0)
        weights = {
            "g_mix": _row(mix_norm_g[li]), "w_in": w_in[li].astype(BF16),
            "conv_w": conv_mix_w[li].astype(F32), "mu": _row(rwkv_mu[li]),
            "w0": _row(rwkv_w0[li]), "w_up": wup_pad.astype(BF16), "a0": _row(rwkv_a0[li]),
            "a_up": aup_pad.astype(BF16), "g_up": rwkv_g_up[li].astype(BF16),
            "k_k": _row(rwkv_k_k[li]), "k_a": _row(rwkv_k_a[li]), "r_k": _row(rwkv_r_k[li]),
            "gn_w": _row(rwkv_gn_w[li]), "gn_b": _row(rwkv_gn_b[li]),
            "w_out": w_out[li].astype(BF16),
            "g_ffn": _row(ffn_norm_g[li]), "ffn_up": ffn_w_up[li].astype(BF16),
            "ffn_cw": ffn_conv_w[li].astype(F32), "ffn_cb": _row(ffn_conv_b[li]),
            "ffn_down": ffn_w_down[li].astype(BF16), "w_proj": ple_w_proj[li].astype(BF16),
            "g_ple": _row(ple_norm_g[li]), "g_gate": _row(ple_gate_norm_g[li]),
            "w_gate": ple_w_gate[li].astype(BF16), "g_fin": _row(final_norm_g),
        }
        names = tuple(weights)
        x = pl.pallas_call(
            functools.partial(_layer_kernel, names=names, ts=ts, cw=cw, rw=rw, dff=dff,
                              n_s=n_s, n_tiles=n_tiles, final=(li == depth - 1)),
            grid=(n_tiles + 1,),
            in_specs=[pl.BlockSpec((1, ts, d), mixer_tile),
                      pl.BlockSpec((1, ts, p.shape[-1]), channel_tile)]
            + [_const_spec(weights[n].shape) for n in names],
            out_specs=pl.BlockSpec((1, ts, d), channel_tile),
            out_shape=jax.ShapeDtypeStruct(x.shape, F32),
            scratch_shapes=[
                pltpu.VMEM(((w_in.shape[-1] - 3 * cw) // LANES, HALO + ts, LANES), F32),
                pltpu.VMEM((cw // LANES, HALO + ts, LANES), F32),
                pltpu.VMEM((rw // GROUP, HEAD, GROUP), F32),
                pltpu.VMEM((2 * dff // LANES, HALO + ts, LANES), F32),
                pltpu.VMEM((2, ts, d), F32),
            ],
            compiler_params=pltpu.CompilerParams(dimension_semantics=("arbitrary",),
                                                 vmem_limit_bytes=VMEM_LIMIT_BYTES),
            name="layer",
        )(x, p[li], *[weights[n] for n in names])
    return x
```

```python
import functools
import math

import jax
import jax.numpy as jnp
from jax import lax
from jax.experimental import pallas as pl
from jax.experimental.pallas import tpu as pltpu

NORM_EPS = 1e-6
GN_EPS = 64e-5
HEAD = 64
WKV_CHUNK = 64
GROUP = 256
HEADS_PER_GROUP = GROUP // HEAD
HALO = 8
LANES = 128
SEQ_TILE = 256
FF_CHUNK = 256
VMEM_LIMIT_BYTES = 56 * 1024 * 1024
WAVE_CHUNKS = 4
CHANNEL_AHEAD = 4

F32 = jnp.float32
BF16 = jnp.bfloat16


def _rms(x, g):
    ms = jnp.mean(x * x, axis=-1, keepdims=True)
    return x * lax.rsqrt(ms + NORM_EPS) * g


def _sigmoid(x):
    return 1.0 / (1.0 + jnp.exp(-x))


def _mm(a, b):
    return jnp.dot(a.astype(BF16), b.astype(BF16), preferred_element_type=F32)


def _mm_split2(a_bf16, b):
    b1 = b.astype(BF16)
    b2 = (b - b1.astype(F32)).astype(BF16)
    return (jnp.dot(a_bf16, b1, preferred_element_type=F32)
            + jnp.dot(a_bf16, b2, preferred_element_type=F32))


def _iota2(shape, dim):
    return lax.broadcasted_iota(jnp.int32, shape, dim)


def _hist_put(hist, tile0, cur):
    ts = cur.shape[0]
    for c in range(cur.shape[1] // LANES):
        hist[tile0 + c, HALO:HALO + ts, :] = cur[:, c * LANES:(c + 1) * LANES]


def _hist_get(hist, tile0, n, ts, delays):
    outs = [jnp.concatenate([hist[tile0 + c, HALO - d:HALO - d + ts, :] for c in range(n)], axis=1)
            for d in delays]
    for c in range(n):
        hist[tile0 + c, 0:HALO, :] = hist[tile0 + c, ts:ts + HALO, :]
    return outs


def _mixer_stream(x, w, zhist, cxhist, s_ref, result, *, ts, cw, rw):
    n_chunks = ts // WKV_CHUNK
    n_groups = rw // GROUP
    C = WKV_CHUNK

    h = _rms(x, w["g_mix"][...]).astype(BF16)

    zc = jnp.dot(h, w["w_in"][:, 0:3 * cw], preferred_element_type=F32)
    x_in = zc[:, 0:cw]
    b_gate = zc[:, cw:2 * cw]
    c_gate = zc[:, 2 * cw:3 * cw]
    cx = c_gate * x_in
    _hist_put(cxhist, 0, cx)
    yield

    zr = jnp.dot(h, w["w_in"][:, 3 * cw:], preferred_element_type=F32)
    _hist_put(zhist, 0, zr)
    yield
    cx1, cx2 = _hist_get(cxhist, 0, cw // LANES, ts, (1, 2))
    conv = cx * w["conv_w"][2:3, :] + cx1 * w["conv_w"][1:2, :] + cx2 * w["conv_w"][0:1, :]
    y_conv = b_gate * conv
    (zs,) = _hist_get(zhist, 0, zr.shape[1] // LANES, ts, (1,))
    zz = zr + (zs - zr) * w["mu"][...]
    yield
    r = zz[:, 0:rw]
    k = zz[:, rw:2 * rw]
    v = zz[:, 2 * rw:3 * rw]
    lora_in = zz[:, 3 * rw:3 * rw + 128]
    gd = zz[:, 3 * rw + 128:3 * rw + 256]

    u_dec = w["w0"][...] + _mm(jnp.tanh(lora_in), w["w_up"][...])
    logw = (-math.exp(-0.5)) * _sigmoid(u_dec)
    a = _sigmoid(w["a0"][...] + _mm(lora_in, w["a_up"][...]))
    g = _mm(_sigmoid(gd), w["g_up"][...])
    yield

    def head_sums(t):
        first = _iota2((ts, LANES), 1) < HEAD
        outs = []
        for c in range(rw // LANES):
            tile = t[:, c * LANES:(c + 1) * LANES]
            left = jnp.sum(jnp.where(first, tile, 0.0), axis=-1, keepdims=True)
            right = jnp.sum(jnp.where(first, 0.0, tile), axis=-1, keepdims=True)
            outs.append(jnp.where(first, left, right))
        return jnp.concatenate(outs, axis=1)

    kk = k * w["k_k"][...]
    kk = kk * lax.rsqrt(jnp.maximum(head_sums(kk * kk), 1e-24))
    k2 = k * (1.0 + (a - 1.0) * w["k_a"][...])
    aa = -kk
    bb = kk * a
    yield

    ri = _iota2((ts, ts), 0)
    ci = _iota2((ts, ts), 1)
    tri = jnp.where(((ri // C) == (ci // C)) & (ci <= ri), 1.0, 0.0).astype(BF16)
    cl = _mm_split2(tri, logw)
    yield
    ec = jnp.exp(cl)
    eci = 1.0 / ec
    ecm = jnp.exp(cl - logw)
    pc = jnp.concatenate(
        [jnp.broadcast_to(ec[(c + 1) * C - 1:(c + 1) * C, :], (C, rw)) for c in range(n_chunks)],
        axis=0)
    yield
    a_t = aa * ecm
    r_t = r * ec
    b_t = bb * eci
    k_t = k2 * eci
    b_h = b_t * pc
    k_h = k_t * pc
    yield

    rs = _iota2((C, GROUP), 0)
    cs = _iota2((C, GROUP), 1) % HEAD
    lower_strict = cs < rs
    lower_incl = cs <= rs
    eye_sbs = jnp.where(cs == rs, 1.0, 0.0).astype(F32)
    pair_head = (_iota2((2 * C, GROUP), 1) % (2 * HEAD)) // HEAD
    half_lane = _iota2((HEAD, LANES), 1) // HEAD

    def bd(y):
        yb = y.astype(BF16)
        zero_tile = jnp.zeros((HEAD, LANES), BF16)
        rows = []
        for hh in range(HEADS_PER_GROUP):
            lt = hh // 2
            piece = jnp.where(half_lane == hh % 2, yb[:, lt * LANES:(lt + 1) * LANES],
                              jnp.zeros((), BF16))
            rows.append(jnp.concatenate([piece, zero_tile] if lt == 0 else [zero_tile, piece],
                                        axis=1))
        return jnp.concatenate(rows, axis=0)

    b_tT = b_t.T
    k_tT = k_t.T

    def bd_t(xt, c, gi):
        lt, hc = c // 2, c % 2
        zero_tile = jnp.zeros((HEAD, LANES), BF16)
        rows = []
        for hh in range(HEADS_PER_GROUP):
            src = xt[gi * GROUP + hh * HEAD:gi * GROUP + (hh + 1) * HEAD,
                     lt * LANES:(lt + 1) * LANES]
            if hc != hh % 2:
                src = pltpu.roll(src, HEAD, axis=1)
            piece = jnp.where(half_lane == hh % 2, src.astype(BF16), jnp.zeros((), BF16))
            rows.append(jnp.concatenate([piece, zero_tile] if hh // 2 == 0 else [zero_tile, piece],
                                        axis=1))
        return jnp.concatenate(rows, axis=0)

    state = [s_ref[gi] for gi in range(n_groups)]
    o_rows = []
    for wave_start in range(0, n_chunks, WAVE_CHUNKS):
        units = [(c, gi) for c in range(wave_start, wave_start + WAVE_CHUNKS)
                 for gi in range(n_groups)]
        nu = len(units)

        def cut(arr):
            return [arr[c * C:(c + 1) * C, gi * GROUP:(gi + 1) * GROUP] for c, gi in units]

        at, rt, bh, kh, vv, pcu = (cut(a_t), cut(r_t), cut(b_h), cut(k_h), cut(v), cut(pc))
        ar = [jnp.concatenate([at[u], rt[u]], axis=0) for u in range(nu)]
        a1 = [_mm(ar[u], bd_t(b_tT, *units[u])) for u in range(nu)]
        a2 = [_mm(ar[u], bd_t(k_tT, *units[u])) for u in range(nu)]
        a_ab = [jnp.where(lower_strict, a1[u][:C], 0.0) for u in range(nu)]
        a_rb = [jnp.where(lower_incl, a1[u][C:], 0.0) for u in range(nu)]
        a_ak = [jnp.where(lower_strict, a2[u][:C], 0.0) for u in range(nu)]
        a_rk = [jnp.where(lower_incl, a2[u][C:], 0.0) for u in range(nu)]
        yield
        av = [_mm(jnp.concatenate([a_ak[u], a_rk[u]], axis=0), bd(vv[u])) for u in range(nu)]
        yield

        lp = a_ab
        tinv = [eye_sbs + a_ab[u] for u in range(nu)]
        n_sq = int(math.log2(C))
        for i in range(n_sq):
            m = [bd(lp[u]) for u in range(nu)]
            if i == 0:
                lp = [_mm(lp[u], m[u]) for u in range(nu)]
            elif i < n_sq - 1:
                st = [_mm(jnp.concatenate([lp[u], tinv[u]], axis=0), m[u]) for u in range(nu)]
                lp = [st[u][:C] for u in range(nu)]
                tinv = [tinv[u] + st[u][C:] for u in range(nu)]
            else:
                tinv = [tinv[u] + _mm(tinv[u], m[u]) for u in range(nu)]
            yield

        tm = [jnp.concatenate([tinv[u], _mm(a_rb[u], bd(tinv[u]))], axis=0) for u in range(nu)]
        yield
        s1 = [_mm(tm[u], bd(at[u])) for u in range(nu)]
        s2 = [_mm(tm[u], bd(av[u][:C])) for u in range(nu)]
        w1 = [s1[u][:C] for u in range(nu)]
        w2 = [s2[u][:C] for u in range(nu)]
        q = [rt[u] + s1[u][C:] for u in range(nu)]
        o_loc = [s2[u][C:] + av[u][C:] for u in range(nu)]
        yield
        g_sbs, h_sbs = [], []
        for u in range(nu):
            xt = jnp.concatenate([bh[u], kh[u]], axis=0).T
            outs = []
            for pair in range(HEADS_PER_GROUP // 2):
                lhs = jnp.concatenate([xt[(2 * pair) * HEAD:(2 * pair + 1) * HEAD],
                                       xt[(2 * pair + 1) * HEAD:(2 * pair + 2) * HEAD]], axis=1)
                lanes = slice(pair * 2 * HEAD, (pair + 1) * 2 * HEAD)
                blk = jnp.concatenate(
                    [jnp.concatenate([w1[u][:, lanes], w2[u][:, lanes]], axis=1),
                     jnp.concatenate([jnp.zeros((C, 2 * HEAD), F32), vv[u][:, lanes]], axis=1)],
                    axis=0).astype(BF16)
                zero = jnp.zeros((), BF16)
                rhs = jnp.concatenate([jnp.where(pair_head == 0, blk, zero),
                                       jnp.where(pair_head == 1, blk, zero)], axis=0)
                outs.append(_mm(lhs, rhs))
            g_sbs.append(jnp.concatenate([o_[:, :2 * HEAD] for o_ in outs], axis=1)
                         + eye_sbs * pcu[u])
            h_sbs.append(jnp.concatenate([o_[:, 2 * HEAD:] for o_ in outs], axis=1))
        yield

        for cw_i in range(WAVE_CHUNKS):
            o_cols = []
            for gi in range(n_groups):
                u = cw_i * n_groups + gi
                qs = _mm(jnp.concatenate([q[u], g_sbs[u]], axis=0), bd(state[gi]))
                o_cols.append(qs[:C] + o_loc[u])
                state[gi] = qs[C:] + h_sbs[u]
            o_rows.append(jnp.concatenate(o_cols, axis=1))
            yield
    for gi in range(n_groups):
        s_ref[gi] = state[gi]
    o = jnp.concatenate(o_rows, axis=0)

    inv_n = 1.0 / HEAD
    mean = head_sums(o) * inv_n
    d = o - mean
    var = head_sums(d * d) * inv_n
    on = d * lax.rsqrt(var + GN_EPS) * w["gn_w"][...] + w["gn_b"][...]
    bonus = head_sums(r * k2 * w["r_k"][...]) * v
    y_rwkv = (on + bonus) * g
    yield

    y = jnp.concatenate([y_conv, y_rwkv], axis=1)
    result["x1"] = x + _mm(y, w["w_out"][...])


def _channel_stream(x, p, w, uhist, result, *, ts, dff, final):
    h = _rms(x, w["g_ffn"][...]).astype(BF16)
    e = _rms(_mm(p, w["w_proj"][...]), w["g_ple"][...])
    n_ff = dff // FF_CHUNK

    def up(j):
        us = []
        for half in range(2):
            lo = half * dff + j * FF_CHUNK
            us.append(jnp.dot(h, w["ffn_up"][:, lo:lo + FF_CHUNK], preferred_element_type=F32))
            _hist_put(uhist, lo // LANES, us[-1])
        return us

    def conv(u, cols):
        u1, u2 = _hist_get(uhist, cols.start // LANES, FF_CHUNK // LANES, ts, (1, 2))
        return (u * w["ffn_cw"][2:3, cols] + u1 * w["ffn_cw"][1:2, cols]
                + u2 * w["ffn_cw"][0:1, cols] + w["ffn_cb"][:, cols])

    def activation(j, u):
        gate, val = [conv(u[half], slice(half * dff + j * FF_CHUNK,
                                         half * dff + (j + 1) * FF_CHUNK)) for half in range(2)]
        return (gate * _sigmoid(gate) * val).astype(BF16)

    acc = x
    u_next = up(0)
    yield
    for j in range(n_ff):
        u = u_next
        if j + 1 < n_ff:
            u_next = up(j + 1)
            yield
        act = activation(j, u)
        yield
        acc = acc + jnp.dot(act, w["ffn_down"][j * FF_CHUNK:(j + 1) * FF_CHUNK, :],
                            preferred_element_type=F32)
        yield
    x2 = acc

    gate = _sigmoid(_mm(_rms(x2, w["g_gate"][...]), w["w_gate"][...]))
    x3 = x2 + gate * e
    result["out"] = _rms(x3, w["g_fin"][...]) if final else x3


def _run_interleaved(gen_a, n_a, gen_b, n_b, ahead):
    done_b = 0
    for ka in range(n_a):
        next(gen_a, None)
        want_b = min(n_b, -(-((ka + 1 + ahead) * n_b) // n_a))
        while done_b < want_b:
            next(gen_b, None)
            done_b += 1
    for gen in (gen_a, gen_b):
        for _ in gen:
            pass


def _layer_kernel(x_ref, p_ref, *refs, names, ts, cw, rw, dff, n_s, n_tiles, final):
    n_w = len(names)
    w = dict(zip(names, refs[:n_w]))
    o_ref = refs[n_w]
    zhist, cxhist, s_ref, uhist, x1buf = refs[n_w + 1:]

    i = pl.program_id(0)
    mix_s = lax.rem(jnp.minimum(i, n_tiles - 1), jnp.int32(n_s))
    ch_s = lax.rem(jnp.maximum(i - 1, 0), jnp.int32(n_s))

    @pl.when(i == 0)
    def _():
        x1buf[...] = jnp.zeros(x1buf.shape, F32)

    @pl.when(mix_s == 0)
    def _():
        zhist[:, 0:HALO, :] = jnp.zeros((zhist.shape[0], HALO, LANES), F32)
        cxhist[:, 0:HALO, :] = jnp.zeros((cxhist.shape[0], HALO, LANES), F32)
        s_ref[...] = jnp.zeros(s_ref.shape, F32)

    @pl.when(ch_s == 0)
    def _():
        uhist[:, 0:HALO, :] = jnp.zeros((uhist.shape[0], HALO, LANES), F32)

    slot = lax.rem(i, jnp.int32(2))
    result = {}
    mixer = _mixer_stream(x_ref[0], w, zhist, cxhist, s_ref, result, ts=ts, cw=cw, rw=rw)
    channel = _channel_stream(x1buf[1 - slot], p_ref[0], w, uhist, result,
                              ts=ts, dff=dff, final=final)
    n_waves = ts // (WKV_CHUNK * WAVE_CHUNKS)
    n_mixer = 10 + n_waves * (5 + int(math.log2(WKV_CHUNK)) + WAVE_CHUNKS)
    n_channel = 1 + 3 * (dff // FF_CHUNK)
    _run_interleaved(mixer, n_mixer, channel, n_channel, CHANNEL_AHEAD)
    x1buf[slot] = result["x1"]
    o_ref[0] = result["out"]


def _const_spec(shape):
    nd = len(shape)
    return pl.BlockSpec(shape, lambda i: (0,) * nd, pipeline_mode=pl.Buffered(1))


def _row(vec):
    return vec.reshape(1, -1).astype(F32)


def kernel(x, p, mix_norm_g, w_in, conv_mix_w, rwkv_mu, rwkv_w0, rwkv_w_up, rwkv_a0, rwkv_a_up, rwkv_g_up, rwkv_k_k, rwkv_k_a, rwkv_r_k, rwkv_gn_w, rwkv_gn_b, w_out, ffn_norm_g, ffn_w_up, ffn_conv_w, ffn_conv_b, ffn_w_down, ple_w_proj, ple_norm_g, ple_gate_norm_g, ple_w_gate, final_norm_g):
    depth = w_in.shape[0]
    bsz, seq, d = x.shape
    cw = conv_mix_w.shape[-1]
    rw = rwkv_w0.shape[-1]
    dff = ffn_w_down.shape[1]
    n_dec, n_aicl, n_gate = rwkv_w_up.shape[1], rwkv_a_up.shape[1], rwkv_g_up.shape[1]
    ts = SEQ_TILE
    assert seq % ts == 0 and ts % (WKV_CHUNK * WAVE_CHUNKS) == 0
    assert rw % GROUP == 0 and dff % FF_CHUNK == 0
    assert n_dec + n_aicl == 128 and n_gate == 128 and rwkv_r_k.shape[-1] == HEAD
    n_s = seq // ts
    n_tiles = bsz * n_s

    def mixer_tile(i):
        t = jnp.minimum(i, n_tiles - 1)
        return (t // n_s, t % n_s, 0)

    def channel_tile(i):
        t = jnp.maximum(i - 1, 0)
        return (t // n_s, t % n_s, 0)

    for li in range(depth):
        wup_pad = jnp.concatenate([rwkv_w_up[li], jnp.zeros((n_aicl, rw), F32)], axis=0)
        aup_pad = jnp.concatenate([jnp.zeros((n_dec, rw), F32), rwkv_a_up[li]], axis=0)
        weights = {
            "g_mix": _row(mix_norm_g[li]), "w_in": w_in[li].astype(BF16),
            "conv_w": conv_mix_w[li].astype(F32), "mu": _row(rwkv_mu[li]),
            "w0": _row(rwkv_w0[li]), "w_up": wup_pad.astype(BF16), "a0": _row(rwkv_a0[li]),
            "a_up": aup_pad.astype(BF16), "g_up": rwkv_g_up[li].astype(BF16),
            "k_k": _row(rwkv_k_k[li]), "k_a": _row(rwkv_k_a[li]), "r_k": _row(rwkv_r_k[li]),
            "gn_w": _row(rwkv_gn_w[li]), "gn_b": _row(rwkv_gn_b[li]),
            "w_out": w_out[li].astype(BF16),
            "g_ffn": _row(ffn_norm_g[li]), "ffn_up": ffn_w_up[li].astype(BF16),
            "ffn_cw": ffn_conv_w[li].astype(F32), "ffn_cb": _row(ffn_conv_b[li]),
            "ffn_down": ffn_w_down[li].astype(BF16), "w_proj": ple_w_proj[li].astype(BF16),
            "g_ple": _row(ple_norm_g[li]), "g_gate": _row(ple_gate_norm_g[li]),
            "w_gate": ple_w_gate[li].astype(BF16), "g_fin": _row(final_norm_g),
        }
        names = tuple(weights)
        x = pl.pallas_call(
            functools.partial(_layer_kernel, names=names, ts=ts, cw=cw, rw=rw, dff=dff,
                              n_s=n_s, n_tiles=n_tiles, final=(li == depth - 1)),
            grid=(n_tiles + 1,),
            in_specs=[pl.BlockSpec((1, ts, d), mixer_tile),
                      pl.BlockSpec((1, ts, p.shape[-1]), channel_tile)]
            + [_const_spec(weights[n].shape) for n in names],
            out_specs=pl.BlockSpec((1, ts, d), channel_tile),
            out_shape=jax.ShapeDtypeStruct(x.shape, F32),
            scratch_shapes=[
                pltpu.VMEM(((w_in.shape[-1] - 3 * cw) // LANES, HALO + ts, LANES), F32),
                pltpu.VMEM((cw // LANES, HALO + ts, LANES), F32),
                pltpu.VMEM((rw // GROUP, HEAD, GROUP), F32),
                pltpu.VMEM((2 * dff // LANES, HALO + ts, LANES), F32),
                pltpu.VMEM((2, ts, d), F32),
            ],
            compiler_params=pltpu.CompilerParams(dimension_semantics=("arbitrary",),
                                                 vmem_limit_bytes=VMEM_LIMIT_BYTES),
            name="layer",
        )(x, p[li], *[weights[n] for n in names])
    return x
```

```python
import functools
import math

import jax
import jax.numpy as jnp
from jax import lax
from jax.experimental import pallas as pl
from jax.experimental.pallas import tpu as pltpu

NORM_EPS = 1e-6
GN_EPS = 64e-5
HEAD = 64
WKV_CHUNK = 64
GROUP = 256
HEADS_PER_GROUP = GROUP // HEAD
HALO = 8
LANES = 128
SEQ_TILE = 256
FF_CHUNK = 256
VMEM_LIMIT_BYTES = 56 * 1024 * 1024
WAVE_CHUNKS = 4
CHANNEL_AHEAD = 4

F32 = jnp.float32
BF16 = jnp.bfloat16


def _rms(x, g):
    ms = jnp.mean(x * x, axis=-1, keepdims=True)
    return x * lax.rsqrt(ms + NORM_EPS) * g


def _sigmoid(x):
    return 1.0 / (1.0 + jnp.exp(-x))


def _mm(a, b):
    return jnp.dot(a.astype(BF16), b.astype(BF16), preferred_element_type=F32)


def _mm_split2(a_bf16, b):
    b1 = b.astype(BF16)
    b2 = (b - b1.astype(F32)).astype(BF16)
    return (jnp.dot(a_bf16, b1, preferred_element_type=F32)
            + jnp.dot(a_bf16, b2, preferred_element_type=F32))


def _iota2(shape, dim):
    return lax.broadcasted_iota(jnp.int32, shape, dim)


def _hist_put(hist, tile0, cur):
    ts = cur.shape[0]
    for c in range(cur.shape[1] // LANES):
        hist[tile0 + c, HALO:HALO + ts, :] = cur[:, c * LANES:(c + 1) * LANES]


def _hist_get(hist, tile0, n, ts, delays):
    outs = [jnp.concatenate([hist[tile0 + c, HALO - d:HALO - d + ts, :] for c in range(n)], axis=1)
            for d in delays]
    for c in range(n):
        hist[tile0 + c, 0:HALO, :] = hist[tile0 + c, ts:ts + HALO, :]
    return outs


def _mixer_stream(x, w, zhist, cxhist, s_ref, result, *, ts, cw, rw):
    n_chunks = ts // WKV_CHUNK
    n_groups = rw // GROUP
    C = WKV_CHUNK

    h = _rms(x, w["g_mix"][...]).astype(BF16)

    zc = jnp.dot(h, w["w_in"][:, 0:3 * cw], preferred_element_type=F32)
    x_in = zc[:, 0:cw]
    b_gate = zc[:, cw:2 * cw]
    c_gate = zc[:, 2 * cw:3 * cw]
    cx = c_gate * x_in
    _hist_put(cxhist, 0, cx)
    yield

    zr = jnp.dot(h, w["w_in"][:, 3 * cw:], preferred_element_type=F32)
    _hist_put(zhist, 0, zr)
    yield
    cx1, cx2 = _hist_get(cxhist, 0, cw // LANES, ts, (1, 2))
    conv = cx * w["conv_w"][2:3, :] + cx1 * w["conv_w"][1:2, :] + cx2 * w["conv_w"][0:1, :]
    y_conv = b_gate * conv
    (zs,) = _hist_get(zhist, 0, zr.shape[1] // LANES, ts, (1,))
    zz = zr + (zs - zr) * w["mu"][...]
    yield
    r = zz[:, 0:rw]
    k = zz[:, rw:2 * rw]
    v = zz[:, 2 * rw:3 * rw]
    lora_in = zz[:, 3 * rw:3 * rw + 128]
    gd = zz[:, 3 * rw + 128:3 * rw + 256]

    u_dec = w["w0"][...] + _mm(jnp.tanh(lora_in), w["w_up"][...])
    logw = (-math.exp(-0.5)) * _sigmoid(u_dec)
    a = _sigmoid(w["a0"][...] + _mm(lora_in, w["a_up"][...]))
    g = _mm(_sigmoid(gd), w["g_up"][...])
    yield

    seg = (_iota2((rw, rw), 0) // HEAD == _iota2((rw, rw), 1) // HEAD).astype(BF16)

    def head_sums(t):
        return _mm(t, seg)

    kk = k * w["k_k"][...]
    kk = kk * lax.rsqrt(jnp.maximum(head_sums(kk * kk), 1e-24))
    k2 = k * (1.0 + (a - 1.0) * w["k_a"][...])
    aa = -kk
    bb = kk * a
    yield

    ri = _iota2((ts, ts), 0)
    ci = _iota2((ts, ts), 1)
    tri = jnp.where(((ri // C) == (ci // C)) & (ci <= ri), 1.0, 0.0).astype(BF16)
    a_t, r_t, b_t, k_t, b_h, k_h, pc = [], [], [], [], [], [], []
    for gi in range(n_groups):
        lanes = slice(gi * GROUP, (gi + 1) * GROUP)
        logw_g = logw[:, lanes]
        cl = _mm_split2(tri, logw_g)
        ec = jnp.exp(cl)
        eci = 1.0 / ec
        ecm = jnp.exp(cl - logw_g)
        pc_g = jnp.concatenate(
            [jnp.broadcast_to(ec[(c + 1) * C - 1:(c + 1) * C, :], (C, GROUP))
             for c in range(n_chunks)], axis=0)
        a_t.append(aa[:, lanes] * ecm)
        r_t.append(r[:, lanes] * ec)
        b_t.append(bb[:, lanes] * eci)
        k_t.append(k2[:, lanes] * eci)
        b_h.append(b_t[-1] * pc_g)
        k_h.append(k_t[-1] * pc_g)
        pc.append(pc_g)
        yield

    rs = _iota2((C, GROUP), 0)
    cs = _iota2((C, GROUP), 1) % HEAD
    lower_strict = cs < rs
    lower_incl = cs <= rs
    eye_sbs = jnp.where(cs == rs, 1.0, 0.0).astype(F32)
    pair_head = (_iota2((2 * C, GROUP), 1) % (2 * HEAD)) // HEAD
    half_lane = _iota2((HEAD, LANES), 1) // HEAD

    def bd(y):
        yb = y.astype(BF16)
        zero_tile = jnp.zeros((HEAD, LANES), BF16)
        rows = []
        for hh in range(HEADS_PER_GROUP):
            lt = hh // 2
            piece = jnp.where(half_lane == hh % 2, yb[:, lt * LANES:(lt + 1) * LANES],
                              jnp.zeros((), BF16))
            rows.append(jnp.concatenate([piece, zero_tile] if lt == 0 else [zero_tile, piece],
                                        axis=1))
        return jnp.concatenate(rows, axis=0)

    b_tT = [t.T for t in b_t]
    k_tT = [t.T for t in k_t]

    def bd_t(xt, c, gi):
        lt, hc = c // 2, c % 2
        zero_tile = jnp.zeros((HEAD, LANES), BF16)
        rows = []
        for hh in range(HEADS_PER_GROUP):
            src = xt[gi][hh * HEAD:(hh + 1) * HEAD, lt * LANES:(lt + 1) * LANES]
            if hc != hh % 2:
                src = pltpu.roll(src, HEAD, axis=1)
            piece = jnp.where(half_lane == hh % 2, src.astype(BF16), jnp.zeros((), BF16))
            rows.append(jnp.concatenate([piece, zero_tile] if hh // 2 == 0 else [zero_tile, piece],
                                        axis=1))
        return jnp.concatenate(rows, axis=0)

    state = [s_ref[gi] for gi in range(n_groups)]
    o_rows = []
    for wave_start in range(0, n_chunks, WAVE_CHUNKS):
        units = [(c, gi) for c in range(wave_start, wave_start + WAVE_CHUNKS)
                 for gi in range(n_groups)]
        nu = len(units)

        def cut(per_group):
            return [per_group[gi][c * C:(c + 1) * C, :] for c, gi in units]

        at, rt, bh, kh, pcu = cut(a_t), cut(r_t), cut(b_h), cut(k_h), cut(pc)
        vv = [v[c * C:(c + 1) * C, gi * GROUP:(gi + 1) * GROUP] for c, gi in units]
        ar = [jnp.concatenate([at[u], rt[u]], axis=0) for u in range(nu)]
        a1 = [_mm(ar[u], bd_t(b_tT, *units[u])) for u in range(nu)]
        a2 = [_mm(ar[u], bd_t(k_tT, *units[u])) for u in range(nu)]
        a_ab = [jnp.where(lower_strict, a1[u][:C], 0.0) for u in range(nu)]
        a_rb = [jnp.where(lower_incl, a1[u][C:], 0.0) for u in range(nu)]
        a_ak = [jnp.where(lower_strict, a2[u][:C], 0.0) for u in range(nu)]
        a_rk = [jnp.where(lower_incl, a2[u][C:], 0.0) for u in range(nu)]
        yield
        av = [_mm(jnp.concatenate([a_ak[u], a_rk[u]], axis=0), bd(vv[u])) for u in range(nu)]
        yield

        lp = a_ab
        tinv = [eye_sbs + a_ab[u] for u in range(nu)]
        n_sq = int(math.log2(C))
        for i in range(n_sq):
            m = [bd(lp[u]) for u in range(nu)]
            if i == 0:
                lp = [_mm(lp[u], m[u]) for u in range(nu)]
            elif i < n_sq - 1:
                st = [_mm(jnp.concatenate([lp[u], tinv[u]], axis=0), m[u]) for u in range(nu)]
                lp = [st[u][:C] for u in range(nu)]
                tinv = [tinv[u] + st[u][C:] for u in range(nu)]
            else:
                tinv = [tinv[u] + _mm(tinv[u], m[u]) for u in range(nu)]
            yield

        tm = [jnp.concatenate([tinv[u], _mm(a_rb[u], bd(tinv[u]))], axis=0) for u in range(nu)]
        yield
        s1 = [_mm(tm[u], bd(at[u])) for u in range(nu)]
        s2 = [_mm(tm[u], bd(av[u][:C])) for u in range(nu)]
        w1 = [s1[u][:C] for u in range(nu)]
        w2 = [s2[u][:C] for u in range(nu)]
        q = [rt[u] + s1[u][C:] for u in range(nu)]
        o_loc = [s2[u][C:] + av[u][C:] for u in range(nu)]
        yield
        g_sbs, h_sbs = [], []
        for u in range(nu):
            xt = jnp.concatenate([bh[u], kh[u]], axis=0).T
            outs = []
            for pair in range(HEADS_PER_GROUP // 2):
                lhs = jnp.concatenate([xt[(2 * pair) * HEAD:(2 * pair + 1) * HEAD],
                                       xt[(2 * pair + 1) * HEAD:(2 * pair + 2) * HEAD]], axis=1)
                lanes = slice(pair * 2 * HEAD, (pair + 1) * 2 * HEAD)
                blk = jnp.concatenate(
                    [jnp.concatenate([w1[u][:, lanes], w2[u][:, lanes]], axis=1),
                     jnp.concatenate([jnp.zeros((C, 2 * HEAD), F32), vv[u][:, lanes]], axis=1)],
                    axis=0).astype(BF16)
                zero = jnp.zeros((), BF16)
                rhs = jnp.concatenate([jnp.where(pair_head == 0, blk, zero),
                                       jnp.where(pair_head == 1, blk, zero)], axis=0)
                outs.append(_mm(lhs, rhs))
            g_sbs.append(jnp.concatenate([o_[:, :2 * HEAD] for o_ in outs], axis=1)
                         + eye_sbs * pcu[u])
            h_sbs.append(jnp.concatenate([o_[:, 2 * HEAD:] for o_ in outs], axis=1))
        yield

        for cw_i in range(WAVE_CHUNKS):
            o_cols = []
            for gi in range(n_groups):
                u = cw_i * n_groups + gi
                qs = _mm(jnp.concatenate([q[u], g_sbs[u]], axis=0), bd(state[gi]))
                o_cols.append(qs[:C] + o_loc[u])
                state[gi] = qs[C:] + h_sbs[u]
            o_rows.append(jnp.concatenate(o_cols, axis=1))
            yield
    for gi in range(n_groups):
        s_ref[gi] = state[gi]
    o = jnp.concatenate(o_rows, axis=0)

    inv_n = 1.0 / HEAD
    mean = head_sums(o) * inv_n
    d = o - mean
    var = head_sums(d * d) * inv_n
    on = d * lax.rsqrt(var + GN_EPS) * w["gn_w"][...] + w["gn_b"][...]
    bonus = head_sums(r * k2 * w["r_k"][...]) * v
    y_rwkv = (on + bonus) * g
    yield

    y = jnp.concatenate([y_conv, y_rwkv], axis=1)
    result["x1"] = x + _mm(y, w["w_out"][...])


def _channel_stream(x, p, w, uhist, result, *, ts, dff, final):
    h = _rms(x, w["g_ffn"][...]).astype(BF16)
    e = _rms(_mm(p, w["w_proj"][...]), w["g_ple"][...])
    n_ff = dff // FF_CHUNK

    def up(j):
        us = []
        for half in range(2):
            lo = half * dff + j * FF_CHUNK
            us.append(jnp.dot(h, w["ffn_up"][:, lo:lo + FF_CHUNK], preferred_element_type=F32))
            _hist_put(uhist, lo // LANES, us[-1])
        return us

    def conv(u, cols):
        u1, u2 = _hist_get(uhist, cols.start // LANES, FF_CHUNK // LANES, ts, (1, 2))
        return (u * w["ffn_cw"][2:3, cols] + u1 * w["ffn_cw"][1:2, cols]
                + u2 * w["ffn_cw"][0:1, cols] + w["ffn_cb"][:, cols])

    def activation(j, u):
        gate, val = [conv(u[half], slice(half * dff + j * FF_CHUNK,
                                         half * dff + (j + 1) * FF_CHUNK)) for half in range(2)]
        return (gate * _sigmoid(gate) * val).astype(BF16)

    acc = x
    u_next = up(0)
    yield
    for j in range(n_ff):
        u = u_next
        if j + 1 < n_ff:
            u_next = up(j + 1)
            yield
        act = activation(j, u)
        yield
        acc = acc + jnp.dot(act, w["ffn_down"][j * FF_CHUNK:(j + 1) * FF_CHUNK, :],
                            preferred_element_type=F32)
        yield
    x2 = acc

    gate = _sigmoid(_mm(_rms(x2, w["g_gate"][...]), w["w_gate"][...]))
    x3 = x2 + gate * e
    result["out"] = _rms(x3, w["g_fin"][...]) if final else x3


def _run_interleaved(gen_a, n_a, gen_b, n_b, ahead):
    done_b = 0
    for ka in range(n_a):
        next(gen_a, None)
        want_b = min(n_b, -(-((ka + 1 + ahead) * n_b) // n_a))
        while done_b < want_b:
            next(gen_b, None)
            done_b += 1
    for gen in (gen_a, gen_b):
        for _ in gen:
            pass


def _layer_kernel(x_ref, p_ref, *refs, names, ts, cw, rw, dff, n_s, n_tiles, final):
    n_w = len(names)
    w = dict(zip(names, refs[:n_w]))
    o_ref = refs[n_w]
    zhist, cxhist, s_ref, uhist, x1buf = refs[n_w + 1:]

    i = pl.program_id(0)
    mix_s = lax.rem(jnp.minimum(i, n_tiles - 1), jnp.int32(n_s))
    ch_s = lax.rem(jnp.maximum(i - 1, 0), jnp.int32(n_s))

    @pl.when(i == 0)
    def _():
        x1buf[...] = jnp.zeros(x1buf.shape, F32)

    @pl.when(mix_s == 0)
    def _():
        zhist[:, 0:HALO, :] = jnp.zeros((zhist.shape[0], HALO, LANES), F32)
        cxhist[:, 0:HALO, :] = jnp.zeros((cxhist.shape[0], HALO, LANES), F32)
        s_ref[...] = jnp.zeros(s_ref.shape, F32)

    @pl.when(ch_s == 0)
    def _():
        uhist[:, 0:HALO, :] = jnp.zeros((uhist.shape[0], HALO, LANES), F32)

    slot = lax.rem(i, jnp.int32(2))
    result = {}
    mixer = _mixer_stream(x_ref[0], w, zhist, cxhist, s_ref, result, ts=ts, cw=cw, rw=rw)
    channel = _channel_stream(x1buf[1 - slot], p_ref[0], w, uhist, result,
                              ts=ts, dff=dff, final=final)
    n_waves = ts // (WKV_CHUNK * WAVE_CHUNKS)
    n_mixer = 7 + rw // GROUP + n_waves * (5 + int(math.log2(WKV_CHUNK)) + WAVE_CHUNKS)
    n_channel = 1 + 3 * (dff // FF_CHUNK)
    _run_interleaved(mixer, n_mixer, channel, n_channel, CHANNEL_AHEAD)
    x1buf[slot] = result["x1"]
    o_ref[0] = result["out"]


def _const_spec(shape):
    nd = len(shape)
    return pl.BlockSpec(shape, lambda i: (0,) * nd, pipeline_mode=pl.Buffered(1))


def _row(vec):
    return vec.reshape(1, -1).astype(F32)


def kernel(x, p, mix_norm_g, w_in, conv_mix_w, rwkv_mu, rwkv_w0, rwkv_w_up, rwkv_a0, rwkv_a_up, rwkv_g_up, rwkv_k_k, rwkv_k_a, rwkv_r_k, rwkv_gn_w, rwkv_gn_b, w_out, ffn_norm_g, ffn_w_up, ffn_conv_w, ffn_conv_b, ffn_w_down, ple_w_proj, ple_norm_g, ple_gate_norm_g, ple_w_gate, final_norm_g):
    depth = w_in.shape[0]
    bsz, seq, d = x.shape
    cw = conv_mix_w.shape[-1]
    rw = rwkv_w0.shape[-1]
    dff = ffn_w_down.shape[1]
    n_dec, n_aicl, n_gate = rwkv_w_up.shape[1], rwkv_a_up.shape[1], rwkv_g_up.shape[1]
    ts = SEQ_TILE
    assert seq % ts == 0 and ts % (WKV_CHUNK * WAVE_CHUNKS) == 0
    assert rw % GROUP == 0 and dff % FF_CHUNK == 0
    assert n_dec + n_aicl == 128 and n_gate == 128 and rwkv_r_k.shape[-1] == HEAD
    n_s = seq // ts
    n_tiles = bsz * n_s

    def mixer_tile(i):
        t = jnp.minimum(i, n_tiles - 1)
        return (t // n_s, t % n_s, 0)

    def channel_tile(i):
        t = jnp.maximum(i - 1, 0)
        return (t // n_s, t % n_s, 0)

    for li in range(depth):
        wup_pad = jnp.concatenate([rwkv_w_up[li], jnp.zeros((n_aicl, rw), F32)], axis=0)
        aup_pad = jnp.concatenate([jnp.zeros((n_dec, rw), F32), rwkv_a_up[li]], axis=0)
        weights = {
            "g_mix": _row(mix_norm_g[li]), "w_in": w_in[li].astype(BF16),
            "conv_w": conv_mix_w[li].astype(F32), "mu": _row(rwkv_mu[li]),
            "w0": _row(rwkv_w0[li]), "w_up": wup_pad.astype(BF16), "a0": _row(rwkv_a0[li]),
            "a_up": aup_pad.astype(BF16), "g_up": rwkv_g_up[li].astype(BF16),
            "k_k": _row(rwkv_k_k[li]), "k_a": _row(rwkv_k_a[li]), "r_k": _row(rwkv_r_k[li]),
            "gn_w": _row(rwkv_gn_w[li]), "gn_b": _row(rwkv_gn_b[li]),
            "w_out": w_out[li].astype(BF16),
            "g_ffn": _row(ffn_norm_g[li]), "ffn_up": ffn_w_up[li].astype(BF16),
            "ffn_cw": ffn_conv_w[li].astype(F32), "ffn_cb": _row(ffn_conv_b[li]),
            "ffn_down": ffn_w_down[li].astype(BF16), "w_proj": ple_w_proj[li].astype(BF16),
            "g_ple": _row(ple_norm_g[li]), "g_gate": _row(ple_gate_norm_g[li]),
            "w_gate": ple_w_gate[li].astype(BF16), "g_fin": _row(final_norm_g),
        }
        names = tuple(weights)
        x = pl.pallas_call(
            functools.partial(_layer_kernel, names=names, ts=ts, cw=cw, rw=rw, dff=dff,
                              n_s=n_s, n_tiles=n_tiles, final=(li == depth - 1)),
            grid=(n_tiles + 1,),
            in_specs=[pl.BlockSpec((1, ts, d), mixer_tile),
                      pl.BlockSpec((1, ts, p.shape[-1]), channel_tile)]
            + [_const_spec(weights[n].shape) for n in names],
            out_specs=pl.BlockSpec((1, ts, d), channel_tile),
            out_shape=jax.ShapeDtypeStruct(x.shape, F32),
            scratch_shapes=[
                pltpu.VMEM(((w_in.shape[-1] - 3 * cw) // LANES, HALO + ts, LANES), F32),
                pltpu.VMEM((cw // LANES, HALO + ts, LANES), F32),
                pltpu.VMEM((rw // GROUP, HEAD, GROUP), F32),
                pltpu.VMEM((2 * dff // LANES, HALO + ts, LANES), F32),
                pltpu.VMEM((2, ts, d), F32),
            ],
            compiler_params=pltpu.CompilerParams(dimension_semantics=("arbitrary",),
                                                 vmem_limit_bytes=VMEM_LIMIT_BYTES),
            name="layer",
        )(x, p[li], *[weights[n] for n in names])
    return x
```
